```python
import jax, jax.numpy as jnp
from jax import lax
import numpy as np

D_MODEL = 4096
BATCH = 4
SEQ = 4096
DEPTH = 2

N_MIXERS = 2
N_ATTN_LAYERS = (DEPTH + 1) // 2
N_CONV_LAYERS = DEPTH // 2
HEAD_DIM = 128
N_Q_HEADS = D_MODEL // HEAD_DIM
N_KV_HEADS = N_Q_HEADS // 4
GQA_GROUP = N_Q_HEADS // N_KV_HEADS
QKV_DIM = (N_Q_HEADS + 2 * N_KV_HEADS) * HEAD_DIM
Q_BLOCK = 128
ROPE_THETA = 10000.0
GRID_W = 64
AXIS_DIM = HEAD_DIM // 2
AXIS_FREQS = AXIS_DIM // 2
CONV_WIDTH = 3
CONV_DIM = D_MODEL
PEER_HEADS = 8
N_KEYS = 128
N_EXPERTS = N_KEYS * N_KEYS
PEER_QUERY_DIM = 256
PEER_HALF = PEER_QUERY_DIM // 2
PEER_TOPK = 16
PEER_SLOTS = PEER_HEADS * PEER_TOPK
PEER_TOKEN_BLOCK = 128
EPS = 1e-6

kernel_name = "hybrid_gqa_shortconv_peer_encoder"


def rms_norm(x, g):
    xf = x.astype(jnp.float32)
    y = xf * lax.rsqrt(jnp.mean(xf * xf, axis=-1, keepdims=True) + EPS)
    return (y * g.astype(jnp.float32)).astype(x.dtype)


def axial_rope_tables(seq_len):
    rows = seq_len // GRID_W
    row_idx = jnp.broadcast_to(jnp.arange(rows, dtype=jnp.float32)[:, None], (rows, GRID_W)).reshape(seq_len)
    col_idx = jnp.broadcast_to(jnp.arange(GRID_W, dtype=jnp.float32)[None, :], (rows, GRID_W)).reshape(seq_len)
    inv_freq = ROPE_THETA ** (-jnp.arange(0, AXIS_DIM, 2, dtype=jnp.float32) / AXIS_DIM)
    ang = jnp.stack([row_idx[:, None] * inv_freq, col_idx[:, None] * inv_freq], axis=1)
    return jnp.cos(ang), jnp.sin(ang)


def apply_axial_rope(x, cos, sin):
    lead = x.shape[:-1]
    xr = x.reshape(lead + (2, 2, AXIS_FREQS))
    x1, x2 = xr[..., 0, :], xr[..., 1, :]
    bshape = (cos.shape[0],) + (1,) * (x.ndim - 3) + (2, AXIS_FREQS)
    c = cos.reshape(bshape).astype(x.dtype)
    s = sin.reshape(bshape).astype(x.dtype)
    out = jnp.stack([x1 * c - x2 * s, x2 * c + x1 * s], axis=-2)
    return out.reshape(x.shape)


def attention_mixer(h, w_qkv, w_o, q_gain, k_gain):
    B, S, _ = h.shape
    qkv = h @ w_qkv
    q, k, v = jnp.split(qkv, [N_Q_HEADS * HEAD_DIM, (N_Q_HEADS + N_KV_HEADS) * HEAD_DIM], axis=-1)
    q = q.reshape(B, S, N_KV_HEADS, GQA_GROUP, HEAD_DIM)
    k = k.reshape(B, S, N_KV_HEADS, HEAD_DIM)
    v = v.reshape(B, S, N_KV_HEADS, HEAD_DIM)
    q = rms_norm(q, q_gain)
    k = rms_norm(k, k_gain)
    cos, sin = axial_rope_tables(S)
    q = apply_axial_rope(q, cos, sin)
    k = apply_axial_rope(k, cos, sin)
    scale = HEAD_DIM ** -0.5
    n_blk = S // Q_BLOCK
    qb = jnp.moveaxis(q.reshape(B, n_blk, Q_BLOCK, N_KV_HEADS, GQA_GROUP, HEAD_DIM), 1, 0)

    def one_block(qblk):
        s = jnp.einsum('bqkgd,bskd->bkgqs', qblk, k).astype(jnp.float32) * scale
        p = jax.nn.softmax(s, axis=-1).astype(v.dtype)
        return jnp.einsum('bkgqs,bskd->bqkgd', p, v)

    o = lax.map(one_block, qb)
    o = jnp.moveaxis(o, 0, 1).reshape(B, S, N_Q_HEADS * HEAD_DIM)
    return o @ w_o


def short_conv_mixer(h, w_in, conv_w, conv_b, w_out):
    S = h.shape[1]
    bcx = h @ w_in
    b_gate, c_gate, xin = jnp.split(bcx, 3, axis=-1)
    u = c_gate * xin
    up = jnp.pad(u, ((0, 0), (1, 1), (0, 0)))
    conv = (up[:, 0:S] * conv_w[0] + up[:, 1:S + 1] * conv_w[1]
            + up[:, 2:S + 2] * conv_w[2] + conv_b)
    return (b_gate * conv) @ w_out


def peer_ffn(h, w_query, sub_keys, expert_u, expert_v):
    B, S, D = h.shape
    T = B * S
    xt = h.reshape(T, D)
    q = (xt @ w_query).reshape(T, PEER_HEADS, 2, PEER_HALF)
    scores = jnp.einsum('thpd,hpnd->thpn', q, sub_keys).astype(jnp.float32)
    top_s, top_i = lax.top_k(scores, PEER_TOPK)
    cand_s = top_s[:, :, 0, :, None] + top_s[:, :, 1, None, :]
    cand_i = top_i[:, :, 0, :, None] * N_KEYS + top_i[:, :, 1, None, :]
    cand_s = cand_s.reshape(T, PEER_HEADS, PEER_TOPK * PEER_TOPK)
    cand_i = cand_i.reshape(T, PEER_HEADS, PEER_TOPK * PEER_TOPK)
    best_s, pos = lax.top_k(cand_s, PEER_TOPK)
    expert_idx = jnp.take_along_axis(cand_i, pos, axis=-1).reshape(T, PEER_SLOTS)
    gates = jax.nn.softmax(best_s, axis=-1).astype(h.dtype).reshape(T, PEER_SLOTS)
    n_blk = T // PEER_TOKEN_BLOCK

    def one_block(args):
        xb, idx, g = args
        u = expert_u[idx]
        act = jax.nn.gelu(jnp.einsum('td,tkd->tk', xb, u), approximate=False)
        v = expert_v[idx]
        return jnp.einsum('tk,tkd->td', g * act, v)

    out = lax.map(one_block, (xt.reshape(n_blk, PEER_TOKEN_BLOCK, D),
                              expert_idx.reshape(n_blk, PEER_TOKEN_BLOCK, PEER_SLOTS),
                              gates.reshape(n_blk, PEER_TOKEN_BLOCK, PEER_SLOTS)))
    return out.reshape(B, S, D)


def setup_inputs(seed: int = 0) -> dict:
    key = jax.random.key(seed)
    ks = jax.random.split(key, 16)
    f32 = jnp.float32
    D = D_MODEL
    nrm = lambda k, shape, s: jax.random.normal(k, shape, f32) * s
    return {
        "x": nrm(ks[0], (BATCH, SEQ, D), 1.0),
        "mixer_norm_g": 1.0 + nrm(ks[1], (DEPTH, D), 0.01),
        "ffn_norm_g": 1.0 + nrm(ks[2], (DEPTH, D), 0.01),
        "attn_w_qkv": nrm(ks[3], (N_ATTN_LAYERS, D, QKV_DIM), D ** -0.5),
        "attn_w_o": nrm(ks[4], (N_ATTN_LAYERS, N_Q_HEADS * HEAD_DIM, D), (N_Q_HEADS * HEAD_DIM) ** -0.5),
        "attn_q_gain": 1.0 + nrm(ks[5], (N_ATTN_LAYERS, HEAD_DIM), 0.01),
        "attn_k_gain": 1.0 + nrm(ks[6], (N_ATTN_LAYERS, HEAD_DIM), 0.01),
        "conv_w_in": nrm(ks[7], (N_CONV_LAYERS, D, 3 * CONV_DIM), D ** -0.5),
        "conv_w": nrm(ks[8], (N_CONV_LAYERS, CONV_WIDTH, CONV_DIM), CONV_WIDTH ** -0.5),
        "conv_b": nrm(ks[9], (N_CONV_LAYERS, CONV_DIM), 0.01),
        "conv_w_out": nrm(ks[10], (N_CONV_LAYERS, CONV_DIM, D), CONV_DIM ** -0.5),
        "peer_w_query": nrm(ks[11], (DEPTH, D, PEER_HEADS * PEER_QUERY_DIM), D ** -0.5),
        "peer_sub_keys": nrm(ks[12], (DEPTH, PEER_HEADS, 2, N_KEYS, PEER_HALF), PEER_HALF ** -0.5),
        "peer_u": nrm(ks[13], (DEPTH, N_EXPERTS, D), D ** -0.5),
        "peer_v": nrm(ks[14], (DEPTH, N_EXPERTS, D), PEER_SLOTS ** -0.5),
    }


def reference(x, mixer_norm_g, ffn_norm_g, attn_w_qkv, attn_w_o, attn_q_gain, attn_k_gain,
              conv_w_in, conv_w, conv_b, conv_w_out, peer_w_query, peer_sub_keys, peer_u, peer_v):
    h = x
    for layer in range(DEPTH):
        hn = rms_norm(h, mixer_norm_g[layer])
        j = layer // N_MIXERS
        if layer % N_MIXERS == 0:
            mix = attention_mixer(hn, attn_w_qkv[j], attn_w_o[j], attn_q_gain[j], attn_k_gain[j])
        else:
            mix = short_conv_mixer(hn, conv_w_in[j], conv_w[j], conv_b[j], conv_w_out[j])
        h = h + mix
        h = h + peer_ffn(rms_norm(h, ffn_norm_g[layer]), peer_w_query[layer], peer_sub_keys[layer],
                         peer_u[layer], peer_v[layer])
    return h
```

```python
import functools

import numpy as np
import jax
import jax.numpy as jnp
from jax import lax
from jax.experimental import pallas as pl
from jax.experimental.pallas import tpu as pltpu

HEAD_DIM = 128
GQA_GROUP = 4
ROPE_THETA = 10000.0
GRID_W = 64
PEER_HEADS = 8
N_KEYS = 128
PEER_HALF = 128
PEER_TOPK = 16
EPS = 1e-6

LANES = 128
VMEM_LIMIT_BYTES = 56 * 2**20
F32 = jnp.float32
BF16 = jnp.bfloat16

_CAND = [(p1, p2) for p1 in range(PEER_TOPK) for p2 in range(PEER_TOPK // (p1 + 1))]
_CAND_ROWS = -(-len(_CAND) // 8) * 8


def _params(*semantics):
    return pltpu.CompilerParams(dimension_semantics=semantics, vmem_limit_bytes=VMEM_LIMIT_BYTES)


def _tile(n, pref):
    t = min(n, pref)
    assert n % t == 0, (n, pref)
    return t


def _rms(x, g):
    ms = jnp.mean(x * x, axis=-1, keepdims=True)
    return x * lax.rsqrt(ms + EPS) * g


def _rmsnorm_body(x_ref, g_ref, o_ref):
    o_ref[...] = _rms(x_ref[...], g_ref[...]).astype(o_ref.dtype)


def _add_rmsnorm_body(a_ref, b_ref, g_ref, h_ref, o_ref):
    h = a_ref[...] + b_ref[...]
    h_ref[...] = h
    o_ref[...] = _rms(h, g_ref[...]).astype(o_ref.dtype)


def _add_body(a_ref, b_ref, o_ref):
    o_ref[...] = a_ref[...] + b_ref[...]


def rmsnorm(x, g):
    t, d = x.shape
    tr = _tile(t, 256)
    row = pl.BlockSpec((tr, d), lambda i: (i, 0))
    return pl.pallas_call(
        _rmsnorm_body,
        grid=(t // tr,),
        in_specs=[row, pl.BlockSpec((1, d), lambda i: (0, 0))],
        out_specs=row,
        out_shape=jax.ShapeDtypeStruct((t, d), BF16),
        compiler_params=_params("parallel"),
        name="rmsnorm",
    )(x, g.reshape(1, d))


def add_rmsnorm(a, b, g):
    t, d = a.shape
    tr = _tile(t, 256)
    row = pl.BlockSpec((tr, d), lambda i: (i, 0))
    return pl.pallas_call(
        _add_rmsnorm_body,
        grid=(t // tr,),
        in_specs=[row, row, pl.BlockSpec((1, d), lambda i: (0, 0))],
        out_specs=[row, row],
        out_shape=[jax.ShapeDtypeStruct((t, d), F32), jax.ShapeDtypeStruct((t, d), BF16)],
        compiler_params=_params("parallel"),
        name="add_rmsnorm",
    )(a, b, g.reshape(1, d))


def add(a, b):
    t, d = a.shape
    tr = _tile(t, 256)
    row = pl.BlockSpec((tr, d), lambda i: (i, 0))
    return pl.pallas_call(
        _add_body,
        grid=(t // tr,),
        in_specs=[row, row],
        out_specs=row,
        out_shape=jax.ShapeDtypeStruct((t, d), F32),
        compiler_params=_params("parallel"),
        name="residual_add",
    )(a, b)


def _rope_tables(seq_len):
    pos = np.arange(seq_len)
    row_idx = (pos // GRID_W).astype(np.float32)
    col_idx = (pos % GRID_W).astype(np.float32)
    axis_dim = HEAD_DIM // 2
    inv_freq = jnp.asarray(ROPE_THETA, F32) ** (-jnp.arange(0, axis_dim, 2, dtype=F32) / axis_dim)
    ang_r = jnp.asarray(row_idx)[:, None] * inv_freq
    ang_c = jnp.asarray(col_idx)[:, None] * inv_freq
    zeros = jnp.zeros_like(ang_r)
    cos = jnp.concatenate([jnp.cos(ang_r)] * 2 + [jnp.cos(ang_c)] * 2, axis=1)
    s_lo = jnp.concatenate([-jnp.sin(ang_r), zeros, -jnp.sin(ang_c), zeros], axis=1)
    s_hi = jnp.concatenate([zeros, jnp.sin(ang_r), zeros, jnp.sin(ang_c)], axis=1)
    return cos, s_lo, s_hi


def _qkv_body(x_ref, w_ref, gp_ref, c_ref, slo_ref, shi_ref, o_ref, *, n_qk_tiles, heads_per_tile):
    j = pl.program_id(1)
    acc = jnp.dot(x_ref[...], w_ref[...], preferred_element_type=F32)

    @pl.when(j < n_qk_tiles)
    def _():
        gain = gp_ref[0:1, :]
        post = gp_ref[1:2, :]
        cos, s_lo, s_hi = c_ref[...], slo_ref[...], shi_ref[...]
        for h in range(heads_per_tile):
            sl = slice(h * HEAD_DIM, (h + 1) * HEAD_DIM)
            y = _rms(acc[:, sl], gain)
            r = (y * cos + pltpu.roll(y, HEAD_DIM - 32, 1) * s_lo + pltpu.roll(y, 32, 1) * s_hi)
            o_ref[:, sl] = (r * post).astype(o_ref.dtype)

    @pl.when(j >= n_qk_tiles)
    def _():
        o_ref[...] = acc.astype(o_ref.dtype)


def qkv_projection(hn, w_qkv, q_gain, k_gain, seq_len):
    t, d = hn.shape
    n = w_qkv.shape[1]
    kv_cols = d // GQA_GROUP
    tm = _tile(seq_len, 1024)
    tn = _tile(kv_cols, 1024)
    n_q_tiles, n_k_tiles = d // tn, kv_cols // tn
    scale = HEAD_DIM ** -0.5
    ones = jnp.ones((HEAD_DIM,), F32)
    rows = ([jnp.stack([q_gain, ones * scale])] * n_q_tiles + [jnp.stack([k_gain, ones])] * n_k_tiles
            + [jnp.stack([ones, ones])] * n_k_tiles)
    gp = jnp.stack(rows)
    cos, s_lo, s_hi = _rope_tables(seq_len)
    seq_tiles = seq_len // tm
    tab = pl.BlockSpec((tm, HEAD_DIM), lambda i, j: (i % seq_tiles, 0))
    body = functools.partial(_qkv_body, n_qk_tiles=n_q_tiles + n_k_tiles, heads_per_tile=tn // HEAD_DIM)
    return pl.pallas_call(
        body,
        grid=(t // tm, n // tn),
        in_specs=[
            pl.BlockSpec((tm, d), lambda i, j: (i, 0)),
            pl.BlockSpec((d, tn), lambda i, j: (0, j)),
            pl.BlockSpec((None, 2, HEAD_DIM), lambda i, j: (j, 0, 0)),
            tab, tab, tab,
        ],
        out_specs=pl.BlockSpec((tm, tn), lambda i, j: (i, j)),
        out_shape=jax.ShapeDtypeStruct((t, n), BF16),
        compiler_params=_params("parallel", "arbitrary"),
        name="qkv_projection",
    )(hn, w_qkv, gp, cos, s_lo, s_hi)


def _attention_body(q_ref, k_ref, v_ref, o_ref):
    k = k_ref[...]
    v = v_ref[...]
    for h in range(GQA_GROUP):
        sl = slice(h * HEAD_DIM, (h + 1) * HEAD_DIM)
        s = lax.dot_general(q_ref[:, sl], k, (((1,), (1,)), ((), ())), preferred_element_type=F32)
        m = jnp.max(s, axis=-1, keepdims=True)
        p = jnp.exp(s - m)
        denom = jnp.sum(p, axis=-1, keepdims=True)
        o = jnp.dot(p.astype(BF16), v, preferred_element_type=F32)
        o_ref[:, sl] = (o / denom).astype(o_ref.dtype)


def attention(qkv, batch, seq_len, d):
    t = qkv.shape[0]
    n_q = d // HEAD_DIM
    n_kv = n_q // GQA_GROUP
    tq = _tile(seq_len, 256)
    q_tiles = seq_len // tq
    gw = GQA_GROUP * HEAD_DIM
    return pl.pallas_call(
        _attention_body,
        grid=(batch, n_kv, q_tiles),
        in_specs=[
            pl.BlockSpec((tq, gw), lambda b, g, i: (b * q_tiles + i, g)),
            pl.BlockSpec((seq_len, HEAD_DIM), lambda b, g, i: (b, n_q + g)),
            pl.BlockSpec((seq_len, HEAD_DIM), lambda b, g, i: (b, n_q + n_kv + g)),
        ],
        out_specs=pl.BlockSpec((tq, gw), lambda b, g, i: (b * q_tiles + i, g)),
        out_shape=jax.ShapeDtypeStruct((t, d), BF16),
        compiler_params=_params("parallel", "parallel", "arbitrary"),
        name="attention",
    )(qkv, qkv, qkv)


def _proj_residual_body(x_ref, w_ref, r_ref, o_ref):
    o_ref[...] = r_ref[...] + jnp.dot(x_ref[...], w_ref[...], preferred_element_type=F32)


def proj_residual(x, w, res):
    t, k = x.shape
    n = w.shape[1]
    tm = _tile(t, 512)
    tn = _tile(n, 1024)
    return pl.pallas_call(
        _proj_residual_body,
        grid=(t // tm, n // tn),
        in_specs=[
            pl.BlockSpec((tm, k), lambda i, j: (i, 0)),
            pl.BlockSpec((k, tn), lambda i, j: (0, j)),
            pl.BlockSpec((tm, tn), lambda i, j: (i, j)),
        ],
        out_specs=pl.BlockSpec((tm, tn), lambda i, j: (i, j)),
        out_shape=jax.ShapeDtypeStruct((t, n), F32),
        compiler_params=_params("parallel", "arbitrary"),
        name="proj_residual",
    )(x, w, res)


def _conv_in_body(x_ref, wb_ref, wc_ref, wx_ref, b_ref, u_ref):
    x = x_ref[...]
    b_ref[...] = jnp.dot(x, wb_ref[...], preferred_element_type=F32).astype(b_ref.dtype)
    c = jnp.dot(x, wc_ref[...], preferred_element_type=F32)
    xin = jnp.dot(x, wx_ref[...], preferred_element_type=F32)
    u_ref[...] = (c * xin).astype(u_ref.dtype)


def conv_in_projection(hn, w_in):
    t, d = hn.shape
    c = w_in.shape[1] // 3
    tm = _tile(t, 512)
    tn = _tile(c, 512)
    nc = c // tn
    out = pl.BlockSpec((tm, tn), lambda i, j: (i, j))
    return pl.pallas_call(
        _conv_in_body,
        grid=(t // tm, nc),
        in_specs=[
            pl.BlockSpec((tm, d), lambda i, j: (i, 0)),
            pl.BlockSpec((d, tn), lambda i, j: (0, j)),
            pl.BlockSpec((d, tn), lambda i, j: (0, nc + j)),
            pl.BlockSpec((d, tn), lambda i, j: (0, 2 * nc + j)),
        ],
        out_specs=[out, out],
        out_shape=[jax.ShapeDtypeStruct((t, c), BF16)] * 2,
        compiler_params=_params("parallel", "arbitrary"),
        name="conv_in_projection",
    )(hn, w_in, w_in, w_in)


def _conv_body(b_ref, u_ref, w_ref, bias_ref, o_ref):
    u = u_ref[...].astype(F32)
    s = u.shape[0]
    pos = lax.broadcasted_iota(jnp.int32, u.shape, 0)
    prev = jnp.where(pos == 0, 0.0, pltpu.roll(u, 1, 0))
    nxt = jnp.where(pos == s - 1, 0.0, pltpu.roll(u, s - 1, 0))
    conv = prev * w_ref[0:1, :] + u * w_ref[1:2, :] + nxt * w_ref[2:3, :] + bias_ref[...]
    o_ref[...] = (b_ref[...].astype(F32) * conv).astype(o_ref.dtype)


def gated_conv(b, u, conv_w, conv_b, seq_len):
    t, c = u.shape
    tc = _tile(c, 256)
    blk = pl.BlockSpec((seq_len, tc), lambda i, j: (i, j))
    return pl.pallas_call(
        _conv_body,
        grid=(t // seq_len, c // tc),
        in_specs=[blk, blk, pl.BlockSpec((3, tc), lambda i, j: (0, j)), pl.BlockSpec((1, tc), lambda i, j: (0, j))],
        out_specs=blk,
        out_shape=jax.ShapeDtypeStruct((t, c), BF16),
        compiler_params=_params("parallel", "parallel"),
        name="gated_conv",
    )(b, u, conv_w, conv_b.reshape(1, c))


def _query_t_body(w_ref, x_ref, o_ref):
    o_ref[...] = lax.dot_general(w_ref[...], x_ref[...], (((1,), (1,)), ((), ())), preferred_element_type=F32)


def peer_query_t(wq_t, hn):
    q, d = wq_t.shape
    t = hn.shape[0]
    tm = _tile(q, 1024)
    tn = _tile(t, 512)
    return pl.pallas_call(
        _query_t_body,
        grid=(q // tm, t // tn),
        in_specs=[pl.BlockSpec((tm, d), lambda i, j: (i, 0)), pl.BlockSpec((tn, d), lambda i, j: (j, 0))],
        out_specs=pl.BlockSpec((tm, tn), lambda i, j: (i, j)),
        out_shape=jax.ShapeDtypeStruct((q, t), F32),
        compiler_params=_params("parallel", "arbitrary"),
        name="peer_query",
    )(wq_t, hn)


def _top16(s, iota):
    work = s
    rank = jnp.full(s.shape, float(PEER_TOPK), F32)
    vals = []
    for p in range(PEER_TOPK):
        m = jnp.max(work, axis=0, keepdims=True)
        idx = jnp.min(jnp.where(work == m, iota, float(N_KEYS)), axis=0, keepdims=True)
        hit = iota == idx
        rank = jnp.where(hit, float(p), rank)
        work = jnp.where(hit, -jnp.inf, work)
        vals.append(m)
    return vals, rank


def _select_body(qt_ref, keys_ref, rank2_ref, lr_ref, e1_ref, e2_ref, cand_ref, sel_ref):
    qt = qt_ref[...].astype(BF16)
    s1 = jnp.dot(keys_ref[0].astype(BF16), qt[:PEER_HALF], preferred_element_type=F32)
    s2 = jnp.dot(keys_ref[1].astype(BF16), qt[PEER_HALF:], preferred_element_type=F32)
    iota = lax.broadcasted_iota(jnp.int32, s1.shape, 0).astype(F32)
    v1, rank1 = _top16(s1, iota)
    v2, rank2 = _top16(s2, iota)

    cand_ref[...] = jnp.full(cand_ref.shape, -jnp.inf, F32)
    for r, (p1, p2) in enumerate(_CAND):
        cand_ref[r:r + 1, :] = v1[p1] + v2[p2]
    cand = cand_ref[...]
    riota = lax.broadcasted_iota(jnp.int32, cand.shape, 0).astype(F32)
    work = cand
    sel = jnp.zeros(cand.shape, F32)
    for _ in range(PEER_TOPK):
        m = jnp.max(work, axis=0, keepdims=True)
        idx = jnp.min(jnp.where(work == m, riota, float(_CAND_ROWS)), axis=0, keepdims=True)
        hit = riota == idx
        sel = jnp.where(hit, 1.0, sel)
        work = jnp.where(hit, -jnp.inf, work)
    z = jnp.sum(sel * jnp.exp(cand - cand[0:1, :]), axis=0, keepdims=True)
    sel_ref[...] = sel

    lr = jnp.zeros(s1.shape, F32)
    r0 = 0
    for p1 in range(PEER_TOPK):
        n = PEER_TOPK // (p1 + 1)
        count = jnp.sum(sel_ref[r0:r0 + n, :], axis=0, keepdims=True)
        lr = jnp.where(rank1 == float(p1), count, lr)
        r0 += n

    rank2_ref[...] = rank2
    lr_ref[...] = lr
    e1_ref[...] = jnp.exp(s1 - v1[0])
    e2_ref[...] = jnp.exp(s2 - v2[0]) / z


def peer_select(q_t, sub_keys):
    t = q_t.shape[1]
    tb = _tile(t, 256)
    out = pl.BlockSpec((None, N_KEYS, tb), lambda h, i: (h, 0, i))
    shape = jax.ShapeDtypeStruct((PEER_HEADS, N_KEYS, t), F32)
    return pl.pallas_call(
        _select_body,
        grid=(PEER_HEADS, t // tb),
        in_specs=[
            pl.BlockSpec((2 * PEER_HALF, tb), lambda h, i: (h, i)),
            pl.BlockSpec((None, 2, N_KEYS, PEER_HALF), lambda h, i: (h, 0, 0, 0)),
        ],
        out_specs=[out] * 4,
        out_shape=[shape] * 4,
        scratch_shapes=[pltpu.VMEM((_CAND_ROWS, tb), F32), pltpu.VMEM((_CAND_ROWS, tb), F32)],
        compiler_params=_params("parallel", "arbitrary"),
        name="peer_select",
    )(q_t, sub_keys)


def _gelu(a):
    return 0.5 * a * (1.0 + lax.erf(a * np.float32(np.sqrt(0.5))))


def _peer_dense_body(xn_ref, u_ref, v_ref, rank2_ref, e2_ref, lr_ref, e1_ref, o_ref, pt_ref, *, n_sub):
    @pl.when(pl.program_id(1) == 0)
    def _():
        o_ref[...] = jnp.zeros(o_ref.shape, F32)

    a_t = lax.dot_general(u_ref[...], xn_ref[...], (((1,), (1,)), ((), ())), preferred_element_type=F32)
    for j in range(n_sub):
        rows = slice(j * N_KEYS, (j + 1) * N_KEYS)
        gate = None
        for h in range(PEER_HEADS):
            picked = rank2_ref[h] < lr_ref[h, j:j + 1, :]
            term = jnp.where(picked, e2_ref[h] * e1_ref[h, j:j + 1, :], 0.0)
            gate = term if gate is None else gate + term
        pt_ref[rows, :] = (gate * _gelu(a_t[rows, :])).astype(BF16)
    o_ref[...] += lax.dot_general(pt_ref[...], v_ref[...], (((0,), (0,)), ((), ())), preferred_element_type=F32)


def peer_dense(hn, u, v, rank2, lr, e1, e2):
    t, d = hn.shape
    n_exp = u.shape[0]
    tb = _tile(t, 512)
    eb = _tile(n_exp, 512)
    n_sub = eb // N_KEYS
    lr4 = lr.reshape(PEER_HEADS, N_KEYS // n_sub, n_sub, t)
    e14 = e1.reshape(PEER_HEADS, N_KEYS // n_sub, n_sub, t)
    once = pl.Buffered(1)
    tok = pl.BlockSpec((PEER_HEADS, N_KEYS, tb), lambda i, e: (0, 0, i), pipeline_mode=once)
    per_tile = pl.BlockSpec((PEER_HEADS, None, n_sub, tb), lambda i, e: (0, e, 0, i))
    body = functools.partial(_peer_dense_body, n_sub=n_sub)
    return pl.pallas_call(
        body,
        grid=(t // tb, n_exp // eb),
        in_specs=[
            pl.BlockSpec((tb, d), lambda i, e: (i, 0), pipeline_mode=once),
            pl.BlockSpec((eb, d), lambda i, e: (e, 0)),
            pl.BlockSpec((eb, d), lambda i, e: (e, 0)),
            tok, tok, per_tile, per_tile,
        ],
        out_specs=pl.BlockSpec((tb, d), lambda i, e: (i, 0)),
        out_shape=jax.ShapeDtypeStruct((t, d), F32),
        scratch_shapes=[pltpu.VMEM((eb, tb), BF16)],
        compiler_params=_params("parallel", "arbitrary"),
        name="peer_dense",
    )(hn, u, v, rank2, e2, lr4, e14)


def peer_ffn(hn, w_query, sub_keys, expert_u, expert_v):
    q_t = peer_query_t(w_query.T.astype(BF16), hn)
    rank2, lr, e1, e2 = peer_select(q_t, sub_keys)
    return peer_dense(hn, expert_u.astype(BF16), expert_v.astype(BF16), rank2, lr, e1, e2)


def kernel(x, mixer_norm_g, ffn_norm_g, attn_w_qkv, attn_w_o, attn_q_gain, attn_k_gain, conv_w_in, conv_w, conv_b,
           conv_w_out, peer_w_query, peer_sub_keys, peer_u, peer_v):
    batch, seq_len, d = x.shape
    h = x.reshape(batch * seq_len, d)
    depth = mixer_norm_g.shape[0]
    hn = rmsnorm(h, mixer_norm_g[0])
    for layer in range(depth):
        j = layer // 2
        if layer % 2 == 0:
            qkv = qkv_projection(hn, attn_w_qkv[j].astype(BF16), attn_q_gain[j], attn_k_gain[j], seq_len)
            o = attention(qkv, batch, seq_len, d)
            h = proj_residual(o, attn_w_o[j].astype(BF16), h)
        else:
            b, u = conv_in_projection(hn, conv_w_in[j].astype(BF16))
            y = gated_conv(b, u, conv_w[j], conv_b[j], seq_len)
            h = proj_residual(y, conv_w_out[j].astype(BF16), h)
        hn = rmsnorm(h, ffn_norm_g[layer])
        ffn = peer_ffn(hn, peer_w_query[layer], peer_sub_keys[layer], peer_u[layer], peer_v[layer])
        if layer + 1 < depth:
            h, hn = add_rmsnorm(h, ffn, mixer_norm_g[layer + 1])
        else:
            h = add(h, ffn)
    return h.reshape(batch, seq_len, d)
```

```python
import functools

import numpy as np
import jax
import jax.numpy as jnp
from jax import lax
from jax.experimental import pallas as pl
from jax.experimental.pallas import tpu as pltpu

HEAD_DIM = 128
GQA_GROUP = 4
ROPE_THETA = 10000.0
GRID_W = 64
PEER_HEADS = 8
N_KEYS = 128
PEER_HALF = 128
PEER_TOPK = 16
EPS = 1e-6

LANES = 128
GATE_ROWS = 16
GATE_LANES = 2 * LANES
VMEM_LIMIT_BYTES = 56 * 2**20
F32 = jnp.float32
BF16 = jnp.bfloat16

_CAND = [(p1, p2) for p1 in range(PEER_TOPK) for p2 in range(PEER_TOPK // (p1 + 1))]
_CAND_ROWS = -(-len(_CAND) // 8) * 8


def _params(*semantics):
    return pltpu.CompilerParams(dimension_semantics=semantics, vmem_limit_bytes=VMEM_LIMIT_BYTES)


def _tile(n, pref):
    t = min(n, pref)
    assert n % t == 0, (n, pref)
    return t


def _rms(x, g):
    ms = jnp.mean(x * x, axis=-1, keepdims=True)
    return x * lax.rsqrt(ms + EPS) * g


def _rmsnorm_body(x_ref, g_ref, o_ref):
    o_ref[...] = _rms(x_ref[...], g_ref[...]).astype(o_ref.dtype)


def _add_rmsnorm_body(a_ref, bt_ref, g_ref, h_ref, o_ref):
    h = a_ref[...] + bt_ref[...].T
    h_ref[...] = h
    o_ref[...] = _rms(h, g_ref[...]).astype(o_ref.dtype)


def _add_body(a_ref, bt_ref, o_ref):
    o_ref[...] = a_ref[...] + bt_ref[...].T


def rmsnorm(x, g):
    t, d = x.shape
    tr = _tile(t, 256)
    row = pl.BlockSpec((tr, d), lambda i: (i, 0))
    return pl.pallas_call(
        _rmsnorm_body,
        grid=(t // tr,),
        in_specs=[row, pl.BlockSpec((1, d), lambda i: (0, 0))],
        out_specs=row,
        out_shape=jax.ShapeDtypeStruct((t, d), BF16),
        compiler_params=_params("parallel"),
        name="rmsnorm",
    )(x, g.reshape(1, d))


def _rmsnorm_t_body(x_ref, g_ref, o_ref):
    o_ref[...] = _rms(x_ref[...], g_ref[...]).T.astype(o_ref.dtype)


def rmsnorm_t(x, g):
    t, d = x.shape
    tr = _tile(t, 256)
    return pl.pallas_call(
        _rmsnorm_t_body,
        grid=(t // tr,),
        in_specs=[pl.BlockSpec((tr, d), lambda i: (i, 0)), pl.BlockSpec((1, d), lambda i: (0, 0))],
        out_specs=pl.BlockSpec((d, tr), lambda i: (0, i)),
        out_shape=jax.ShapeDtypeStruct((d, t), BF16),
        compiler_params=_params("parallel"),
        name="rmsnorm_t",
    )(x, g.reshape(1, d))


def add_rmsnorm(a, b_t, g):
    t, d = a.shape
    tr = _tile(t, 256)
    row = pl.BlockSpec((tr, d), lambda i: (i, 0))
    col = pl.BlockSpec((d, tr), lambda i: (0, i))
    return pl.pallas_call(
        _add_rmsnorm_body,
        grid=(t // tr,),
        in_specs=[row, col, pl.BlockSpec((1, d), lambda i: (0, 0))],
        out_specs=[row, row],
        out_shape=[jax.ShapeDtypeStruct((t, d), F32), jax.ShapeDtypeStruct((t, d), BF16)],
        compiler_params=_params("parallel"),
        name="add_rmsnorm",
    )(a, b_t, g.reshape(1, d))


def add(a, b_t):
    t, d = a.shape
    tr = _tile(t, 256)
    row = pl.BlockSpec((tr, d), lambda i: (i, 0))
    return pl.pallas_call(
        _add_body,
        grid=(t // tr,),
        in_specs=[row, pl.BlockSpec((d, tr), lambda i: (0, i))],
        out_specs=row,
        out_shape=jax.ShapeDtypeStruct((t, d), F32),
        compiler_params=_params("parallel"),
        name="residual_add",
    )(a, b_t)


def _rope_tables(seq_len):
    pos = np.arange(seq_len)
    row_idx = (pos // GRID_W).astype(np.float32)
    col_idx = (pos % GRID_W).astype(np.float32)
    axis_dim = HEAD_DIM // 2
    inv_freq = jnp.asarray(ROPE_THETA, F32) ** (-jnp.arange(0, axis_dim, 2, dtype=F32) / axis_dim)
    ang_r = jnp.asarray(row_idx)[:, None] * inv_freq
    ang_c = jnp.asarray(col_idx)[:, None] * inv_freq
    zeros = jnp.zeros_like(ang_r)
    cos = jnp.concatenate([jnp.cos(ang_r)] * 2 + [jnp.cos(ang_c)] * 2, axis=1)
    s_lo = jnp.concatenate([-jnp.sin(ang_r), zeros, -jnp.sin(ang_c), zeros], axis=1)
    s_hi = jnp.concatenate([zeros, jnp.sin(ang_r), zeros, jnp.sin(ang_c)], axis=1)
    return cos, s_lo, s_hi


def _qkv_body(x_ref, w_ref, gp_ref, c_ref, slo_ref, shi_ref, o_ref, *, n_qk_tiles, heads_per_tile):
    j = pl.program_id(1)
    acc = jnp.dot(x_ref[...], w_ref[...], preferred_element_type=F32)

    @pl.when(j < n_qk_tiles)
    def _():
        gain = gp_ref[0:1, :]
        post = gp_ref[1:2, :]
        cos, s_lo, s_hi = c_ref[...], slo_ref[...], shi_ref[...]
        for h in range(heads_per_tile):
            sl = slice(h * HEAD_DIM, (h + 1) * HEAD_DIM)
            y = _rms(acc[:, sl], gain)
            r = (y * cos + pltpu.roll(y, HEAD_DIM - 32, 1) * s_lo + pltpu.roll(y, 32, 1) * s_hi)
            o_ref[:, sl] = (r * post).astype(o_ref.dtype)

    @pl.when(j >= n_qk_tiles)
    def _():
        o_ref[...] = acc.astype(o_ref.dtype)


def qkv_projection(hn, w_qkv, q_gain, k_gain, seq_len):
    t, d = hn.shape
    n = w_qkv.shape[1]
    kv_cols = d // GQA_GROUP
    tm = _tile(seq_len, 1024)
    tn = _tile(kv_cols, 1024)
    n_q_tiles, n_k_tiles = d // tn, kv_cols // tn
    scale = HEAD_DIM ** -0.5 * np.log2(np.e)
    ones = jnp.ones((HEAD_DIM,), F32)
    rows = ([jnp.stack([q_gain, ones * scale])] * n_q_tiles + [jnp.stack([k_gain, ones])] * n_k_tiles
            + [jnp.stack([ones, ones])] * n_k_tiles)
    gp = jnp.stack(rows)
    cos, s_lo, s_hi = _rope_tables(seq_len)
    seq_tiles = seq_len // tm
    tab = pl.BlockSpec((tm, HEAD_DIM), lambda i, j: (i % seq_tiles, 0))
    body = functools.partial(_qkv_body, n_qk_tiles=n_q_tiles + n_k_tiles, heads_per_tile=tn // HEAD_DIM)
    return pl.pallas_call(
        body,
        grid=(t // tm, n // tn),
        in_specs=[
            pl.BlockSpec((tm, d), lambda i, j: (i, 0)),
            pl.BlockSpec((d, tn), lambda i, j: (0, j)),
            pl.BlockSpec((None, 2, HEAD_DIM), lambda i, j: (j, 0, 0)),
            tab, tab, tab,
        ],
        out_specs=pl.BlockSpec((tm, tn), lambda i, j: (i, j)),
        out_shape=jax.ShapeDtypeStruct((t, n), BF16),
        compiler_params=_params("parallel", "arbitrary"),
        name="qkv_projection",
    )(hn, w_qkv, gp, cos, s_lo, s_hi)


def _attention_body(q_ref, k_ref, v_ref, o_ref, vx_ref, *, rows_per_chain):
    @pl.when(pl.program_id(2) == 0)
    def _():
        vx_ref[:, 0:HEAD_DIM] = v_ref[...]
        vx_ref[:, HEAD_DIM:2 * HEAD_DIM] = jnp.ones(v_ref.shape, BF16)

    k = k_ref[...]
    for r0 in range(0, q_ref.shape[0], rows_per_chain):
        rows = slice(r0, r0 + rows_per_chain)
        for h in range(GQA_GROUP):
            sl = slice(h * HEAD_DIM, (h + 1) * HEAD_DIM)
            s = lax.dot_general(q_ref[rows, sl], k, (((1,), (1,)), ((), ())), preferred_element_type=F32)
            m = jnp.max(s, axis=-1, keepdims=True)
            p = jnp.exp2(s - m).astype(BF16)
            o = jnp.dot(p, vx_ref[...], preferred_element_type=F32)
            o_ref[rows, sl] = (o[:, 0:HEAD_DIM] / o[:, HEAD_DIM:2 * HEAD_DIM]).astype(o_ref.dtype)


def attention(qkv, batch, seq_len, d):
    t = qkv.shape[0]
    n_q = d // HEAD_DIM
    n_kv = n_q // GQA_GROUP
    tq = _tile(seq_len, 512)
    q_tiles = seq_len // tq
    gw = GQA_GROUP * HEAD_DIM
    body = functools.partial(_attention_body, rows_per_chain=_tile(tq, 128))
    return pl.pallas_call(
        body,
        grid=(batch, n_kv, q_tiles),
        in_specs=[
            pl.BlockSpec((tq, gw), lambda b, g, i: (b * q_tiles + i, g)),
            pl.BlockSpec((seq_len, HEAD_DIM), lambda b, g, i: (b, n_q + g)),
            pl.BlockSpec((seq_len, HEAD_DIM), lambda b, g, i: (b, n_q + n_kv + g)),
        ],
        out_specs=pl.BlockSpec((tq, gw), lambda b, g, i: (b * q_tiles + i, g)),
        out_shape=jax.ShapeDtypeStruct((t, d), BF16),
        scratch_shapes=[pltpu.VMEM((seq_len, 2 * HEAD_DIM), BF16)],
        compiler_params=_params("parallel", "parallel", "arbitrary"),
        name="attention",
    )(qkv, qkv, qkv)


def _proj_residual_body(x_ref, w_ref, r_ref, o_ref):
    o_ref[...] = r_ref[...] + jnp.dot(x_ref[...], w_ref[...], preferred_element_type=F32)


def proj_residual(x, w, res):
    t, k = x.shape
    n = w.shape[1]
    tm = _tile(t, 512)
    tn = _tile(n, 1024)
    return pl.pallas_call(
        _proj_residual_body,
        grid=(t // tm, n // tn),
        in_specs=[
            pl.BlockSpec((tm, k), lambda i, j: (i, 0)),
            pl.BlockSpec((k, tn), lambda i, j: (0, j)),
            pl.BlockSpec((tm, tn), lambda i, j: (i, j)),
        ],
        out_specs=pl.BlockSpec((tm, tn), lambda i, j: (i, j)),
        out_shape=jax.ShapeDtypeStruct((t, n), F32),
        compiler_params=_params("parallel", "arbitrary"),
        name="proj_residual",
    )(x, w, res)


def _conv_in_body(x_ref, wb_ref, wc_ref, wx_ref, b_ref, u_ref):
    x = x_ref[...]
    b_ref[...] = jnp.dot(x, wb_ref[...], preferred_element_type=F32).astype(b_ref.dtype)
    c = jnp.dot(x, wc_ref[...], preferred_element_type=F32)
    xin = jnp.dot(x, wx_ref[...], preferred_element_type=F32)
    u_ref[...] = (c * xin).astype(u_ref.dtype)


def conv_in_projection(hn, w_in):
    t, d = hn.shape
    c = w_in.shape[1] // 3
    tm = _tile(t, 512)
    tn = _tile(c, 512)
    nc = c // tn
    out = pl.BlockSpec((tm, tn), lambda i, j: (i, j))
    return pl.pallas_call(
        _conv_in_body,
        grid=(t // tm, nc),
        in_specs=[
            pl.BlockSpec((tm, d), lambda i, j: (i, 0)),
            pl.BlockSpec((d, tn), lambda i, j: (0, j)),
            pl.BlockSpec((d, tn), lambda i, j: (0, nc + j)),
            pl.BlockSpec((d, tn), lambda i, j: (0, 2 * nc + j)),
        ],
        out_specs=[out, out],
        out_shape=[jax.ShapeDtypeStruct((t, c), BF16)] * 2,
        compiler_params=_params("parallel", "arbitrary"),
        name="conv_in_projection",
    )(hn, w_in, w_in, w_in)


def _conv_body(b_ref, u_ref, w_ref, bias_ref, o_ref):
    u = u_ref[...].astype(F32)
    s = u.shape[0]
    pos = lax.broadcasted_iota(jnp.int32, u.shape, 0)
    prev = jnp.where(pos == 0, 0.0, pltpu.roll(u, 1, 0))
    nxt = jnp.where(pos == s - 1, 0.0, pltpu.roll(u, s - 1, 0))
    conv = prev * w_ref[0:1, :] + u * w_ref[1:2, :] + nxt * w_ref[2:3, :] + bias_ref[...]
    o_ref[...] = (b_ref[...].astype(F32) * conv).astype(o_ref.dtype)


def gated_conv(b, u, conv_w, conv_b, seq_len):
    t, c = u.shape
    tc = _tile(c, 256)
    blk = pl.BlockSpec((seq_len, tc), lambda i, j: (i, j))
    return pl.pallas_call(
        _conv_body,
        grid=(t // seq_len, c // tc),
        in_specs=[blk, blk, pl.BlockSpec((3, tc), lambda i, j: (0, j)), pl.BlockSpec((1, tc), lambda i, j: (0, j))],
        out_specs=blk,
        out_shape=jax.ShapeDtypeStruct((t, c), BF16),
        compiler_params=_params("parallel", "parallel"),
        name="gated_conv",
    )(b, u, conv_w, conv_b.reshape(1, c))


def _query_t_body(w_ref, x_ref, o_ref):
    o_ref[...] = jnp.dot(w_ref[...], x_ref[...], preferred_element_type=F32)


def peer_query_t(wq_t, hn_t):
    q, d = wq_t.shape
    t = hn_t.shape[1]
    tm = _tile(q, 1024)
    tn = _tile(t, 512)
    return pl.pallas_call(
        _query_t_body,
        grid=(q // tm, t // tn),
        in_specs=[pl.BlockSpec((tm, d), lambda i, j: (i, 0)), pl.BlockSpec((d, tn), lambda i, j: (0, j))],
        out_specs=pl.BlockSpec((tm, tn), lambda i, j: (i, j)),
        out_shape=jax.ShapeDtypeStruct((q, t), F32),
        compiler_params=_params("parallel", "arbitrary"),
        name="peer_query",
    )(wq_t, hn_t)


def _top16(s, iota):
    work = s
    rank = jnp.full(s.shape, float(PEER_TOPK), F32)
    vals = []
    for p in range(PEER_TOPK):
        m = jnp.max(work, axis=0, keepdims=True)
        idx = jnp.min(jnp.where(work == m, iota, float(N_KEYS)), axis=0, keepdims=True)
        hit = iota == idx
        rank = jnp.where(hit, float(p), rank)
        work = jnp.where(hit, -jnp.inf, work)
        vals.append(m)
    return vals, rank


def _select_body(qt_ref, keys_ref, rank2_ref, lr_ref, e1_ref, e2_ref, cand_ref, sel_ref):
    qt = qt_ref[...].astype(BF16)
    s1 = jnp.dot(keys_ref[0].astype(BF16), qt[:PEER_HALF], preferred_element_type=F32)
    s2 = jnp.dot(keys_ref[1].astype(BF16), qt[PEER_HALF:], preferred_element_type=F32)
    iota = lax.broadcasted_iota(jnp.int32, s1.shape, 0).astype(F32)
    v1, rank1 = _top16(s1, iota)
    v2, rank2 = _top16(s2, iota)

    cand_ref[...] = jnp.full(cand_ref.shape, -jnp.inf, F32)
    for r, (p1, p2) in enumerate(_CAND):
        cand_ref[r:r + 1, :] = v1[p1] + v2[p2]
    cand = cand_ref[...]
    riota = lax.broadcasted_iota(jnp.int32, cand.shape, 0).astype(F32)
    work = cand
    sel = jnp.zeros(cand.shape, F32)
    for _ in range(PEER_TOPK):
        m = jnp.max(work, axis=0, keepdims=True)
        idx = jnp.min(jnp.where(work == m, riota, float(_CAND_ROWS)), axis=0, keepdims=True)
        hit = riota == idx
        sel = jnp.where(hit, 1.0, sel)
        work = jnp.where(hit, -jnp.inf, work)
    z = jnp.sum(sel * jnp.exp(cand - cand[0:1, :]), axis=0, keepdims=True)
    sel_ref[...] = sel

    lr = jnp.zeros(s1.shape, F32)
    r0 = 0
    for p1 in range(PEER_TOPK):
        n = PEER_TOPK // (p1 + 1)
        count = jnp.sum(sel_ref[r0:r0 + n, :], axis=0, keepdims=True)
        lr = jnp.where(rank1 == float(p1), count, lr)
        r0 += n

    lr_ref[...] = lr
    e1_ref[...] = jnp.exp(s1 - v1[0])
    rank2 = rank2.astype(BF16)
    e2 = (jnp.exp(s2 - v2[0]) / z).astype(BF16)
    for r in range(N_KEYS // GATE_ROWS):
        rows = slice(r * GATE_ROWS, (r + 1) * GATE_ROWS)
        rank2_ref[r] = rank2[rows, :]
        e2_ref[r] = e2[rows, :]


def peer_select(q_t, sub_keys):
    t = q_t.shape[1]
    tb = GATE_LANES
    assert t % tb == 0
    n_r = N_KEYS // GATE_ROWS
    row_out = pl.BlockSpec((None, N_KEYS, tb), lambda h, i: (h, 0, i))
    row_shape = jax.ShapeDtypeStruct((PEER_HEADS, N_KEYS, t), F32)
    tile_out = pl.BlockSpec((None, n_r, None, GATE_ROWS, tb), lambda h, i: (i, 0, h, 0, 0))
    tile_shape = jax.ShapeDtypeStruct((t // tb, n_r, PEER_HEADS, GATE_ROWS, tb), BF16)
    return pl.pallas_call(
        _select_body,
        grid=(PEER_HEADS, t // tb),
        in_specs=[
            pl.BlockSpec((2 * PEER_HALF, tb), lambda h, i: (h, i)),
            pl.BlockSpec((None, 2, N_KEYS, PEER_HALF), lambda h, i: (h, 0, 0, 0)),
        ],
        out_specs=[tile_out, row_out, row_out, tile_out],
        out_shape=[tile_shape, row_shape, row_shape, tile_shape],
        scratch_shapes=[pltpu.VMEM((_CAND_ROWS, tb), F32), pltpu.VMEM((_CAND_ROWS, tb), F32)],
        compiler_params=_params("parallel", "arbitrary"),
        name="peer_select",
    )(q_t, sub_keys)


def _gelu(a):
    return 0.5 * a * (1.0 + lax.erf(a * np.float32(np.sqrt(0.5))))


def _gate_tile(rank2_ref, e2_ref, lr_ref, e1_ref, g_ref):
    zero = jnp.zeros((), BF16)
    piece = (GATE_ROWS, GATE_LANES)
    for c in range(g_ref.shape[1] // GATE_LANES):
        cols = slice(c * GATE_LANES, (c + 1) * GATE_LANES)
        for j in range(g_ref.shape[0] // N_KEYS):
            lrows = [jnp.broadcast_to(lr_ref[h, j:j + 1, cols], piece).astype(BF16) for h in range(PEER_HEADS)]
            erows = [jnp.broadcast_to(e1_ref[h, j:j + 1, cols], piece).astype(BF16) for h in range(PEER_HEADS)]
            for r in range(N_KEYS // GATE_ROWS):
                gate = None
                for h in range(PEER_HEADS):
                    term = jnp.where(rank2_ref[c, r, h] < lrows[h], e2_ref[c, r, h] * erows[h], zero)
                    gate = term if gate is None else gate + term
                g_ref[j * N_KEYS + r * GATE_ROWS:j * N_KEYS + (r + 1) * GATE_ROWS, cols] = gate


def _peer_dense_body(xn_ref, u_ref, vt_ref, rank2_ref, e2_ref, lr_ref, e1_ref, o_ref, g_ref, pt_ref):
    @pl.when(pl.program_id(1) == 0)
    def _():
        o_ref[...] = jnp.zeros(o_ref.shape, F32)

    _gate_tile(rank2_ref, e2_ref, lr_ref, e1_ref, g_ref)
    a_t = jnp.dot(u_ref[...], xn_ref[...], preferred_element_type=F32)
    pt_ref[...] = g_ref[...] * _gelu(a_t).astype(BF16)
    o_ref[...] += jnp.dot(vt_ref[...], pt_ref[...], preferred_element_type=F32)


def peer_dense_t(hn_t, u, v_t, rank2, lr, e1, e2):
    d, t = hn_t.shape
    n_exp = u.shape[0]
    tb = _tile(t, 512)
    eb = _tile(n_exp, 512)
    n_sub = eb // N_KEYS
    lr4 = lr.reshape(PEER_HEADS, N_KEYS // n_sub, n_sub, t)
    e14 = e1.reshape(PEER_HEADS, N_KEYS // n_sub, n_sub, t)
    once = pl.Buffered(1)
    tok = pl.BlockSpec((tb // GATE_LANES,) + rank2.shape[1:], lambda i, e: (i, 0, 0, 0, 0), pipeline_mode=once)
    per_tile = pl.BlockSpec((PEER_HEADS, None, n_sub, tb), lambda i, e: (0, e, 0, i))
    return pl.pallas_call(
        _peer_dense_body,
        grid=(t // tb, n_exp // eb),
        in_specs=[
            pl.BlockSpec((d, tb), lambda i, e: (0, i), pipeline_mode=once),
            pl.BlockSpec((eb, d), lambda i, e: (e, 0)),
            pl.BlockSpec((d, eb), lambda i, e: (0, e)),
            tok, tok, per_tile, per_tile,
        ],
        out_specs=pl.BlockSpec((d, tb), lambda i, e: (0, i)),
        out_shape=jax.ShapeDtypeStruct((d, t), F32),
        scratch_shapes=[pltpu.VMEM((eb, tb), BF16), pltpu.VMEM((eb, tb), BF16)],
        compiler_params=_params("parallel", "arbitrary"),
        name="peer_dense",
    )(hn_t, u, v_t, rank2, e2, lr4, e14)


def peer_ffn_t(hn_t, w_query, sub_keys, expert_u, expert_v):
    q_t = peer_query_t(w_query.T.astype(BF16), hn_t)
    rank2, lr, e1, e2 = peer_select(q_t, sub_keys)
    return peer_dense_t(hn_t, expert_u.astype(BF16), expert_v.T.astype(BF16), rank2, lr, e1, e2)


def kernel(x, mixer_norm_g, ffn_norm_g, attn_w_qkv, attn_w_o, attn_q_gain, attn_k_gain, conv_w_in, conv_w, conv_b,
           conv_w_out, peer_w_query, peer_sub_keys, peer_u, peer_v):
    batch, seq_len, d = x.shape
    h = x.reshape(batch * seq_len, d)
    depth = mixer_norm_g.shape[0]
    hn = rmsnorm(h, mixer_norm_g[0])
    for layer in range(depth):
        j = layer // 2
        if layer % 2 == 0:
            qkv = qkv_projection(hn, attn_w_qkv[j].astype(BF16), attn_q_gain[j], attn_k_gain[j], seq_len)
            o = attention(qkv, batch, seq_len, d)
            h = proj_residual(o, attn_w_o[j].astype(BF16), h)
        else:
            b, u = conv_in_projection(hn, conv_w_in[j].astype(BF16))
            y = gated_conv(b, u, conv_w[j], conv_b[j], seq_len)
            h = proj_residual(y, conv_w_out[j].astype(BF16), h)
        hn_t = rmsnorm_t(h, ffn_norm_g[layer])
        ffn_t = peer_ffn_t(hn_t, peer_w_query[layer], peer_sub_keys[layer], peer_u[layer], peer_v[layer])
        if layer + 1 < depth:
            h, hn = add_rmsnorm(h, ffn_t, mixer_norm_g[layer + 1])
        else:
            h = add(h, ffn_t)
    return h.reshape(batch, seq_len, d)
```

```python
import functools

import numpy as np
import jax
import jax.numpy as jnp
from jax import lax
from jax.experimental import pallas as pl
from jax.experimental.pallas import tpu as pltpu

HEAD_DIM = 128
GQA_GROUP = 4
ROPE_THETA = 10000.0
GRID_W = 64
PEER_HEADS = 8
N_KEYS = 128
PEER_HALF = 128
PEER_TOPK = 16
EPS = 1e-6

LANES = 128
GATE_ROWS = 16
GATE_LANES = 2 * LANES
VMEM_LIMIT_BYTES = 56 * 2**20
F32 = jnp.float32
BF16 = jnp.bfloat16

_CAND = [(p1, p2) for p1 in range(PEER_TOPK) for p2 in range(PEER_TOPK // (p1 + 1))]
_CAND_ROWS = -(-len(_CAND) // 8) * 8


def _params(*semantics):
    return pltpu.CompilerParams(dimension_semantics=semantics, vmem_limit_bytes=VMEM_LIMIT_BYTES)


def _tile(n, pref):
    t = min(n, pref)
    assert n % t == 0, (n, pref)
    return t


def _rms(x, g):
    ms = jnp.mean(x * x, axis=-1, keepdims=True)
    return x * lax.rsqrt(ms + EPS) * g


def _rmsnorm_body(x_ref, g_ref, o_ref):
    o_ref[...] = _rms(x_ref[...], g_ref[...]).astype(o_ref.dtype)


def _add_rmsnorm_body(a_ref, bt_ref, g_ref, h_ref, o_ref):
    h = a_ref[...] + bt_ref[...].T
    h_ref[...] = h
    o_ref[...] = _rms(h, g_ref[...]).astype(o_ref.dtype)


def _add_body(a_ref, bt_ref, o_ref):
    o_ref[...] = a_ref[...] + bt_ref[...].T


def rmsnorm(x, g):
    t, d = x.shape
    tr = _tile(t, 256)
    row = pl.BlockSpec((tr, d), lambda i: (i, 0))
    return pl.pallas_call(
        _rmsnorm_body,
        grid=(t // tr,),
        in_specs=[row, pl.BlockSpec((1, d), lambda i: (0, 0))],
        out_specs=row,
        out_shape=jax.ShapeDtypeStruct((t, d), BF16),
        compiler_params=_params("parallel"),
        name="rmsnorm",
    )(x, g.reshape(1, d))


def _rmsnorm_t_body(x_ref, g_ref, o_ref):
    o_ref[...] = _rms(x_ref[...], g_ref[...]).T.astype(o_ref.dtype)


def rmsnorm_t(x, g):
    t, d = x.shape
    tr = _tile(t, 256)
    return pl.pallas_call(
        _rmsnorm_t_body,
        grid=(t // tr,),
        in_specs=[pl.BlockSpec((tr, d), lambda i: (i, 0)), pl.BlockSpec((1, d), lambda i: (0, 0))],
        out_specs=pl.BlockSpec((d, tr), lambda i: (0, i)),
        out_shape=jax.ShapeDtypeStruct((d, t), BF16),
        compiler_params=_params("parallel"),
        name="rmsnorm_t",
    )(x, g.reshape(1, d))


def add_rmsnorm(a, b_t, g):
    t, d = a.shape
    tr = _tile(t, 256)
    row = pl.BlockSpec((tr, d), lambda i: (i, 0))
    col = pl.BlockSpec((d, tr), lambda i: (0, i))
    return pl.pallas_call(
        _add_rmsnorm_body,
        grid=(t // tr,),
        in_specs=[row, col, pl.BlockSpec((1, d), lambda i: (0, 0))],
        out_specs=[row, row],
        out_shape=[jax.ShapeDtypeStruct((t, d), F32), jax.ShapeDtypeStruct((t, d), BF16)],
        compiler_params=_params("parallel"),
        name="add_rmsnorm",
    )(a, b_t, g.reshape(1, d))


def add(a, b_t):
    t, d = a.shape
    tr = _tile(t, 256)
    row = pl.BlockSpec((tr, d), lambda i: (i, 0))
    return pl.pallas_call(
        _add_body,
        grid=(t // tr,),
        in_specs=[row, pl.BlockSpec((d, tr), lambda i: (0, i))],
        out_specs=row,
        out_shape=jax.ShapeDtypeStruct((t, d), F32),
        compiler_params=_params("parallel"),
        name="residual_add",
    )(a, b_t)


def _rope_tables(seq_len):
    pos = np.arange(seq_len)
    row_idx = (pos // GRID_W).astype(np.float32)
    col_idx = (pos % GRID_W).astype(np.float32)
    axis_dim = HEAD_DIM // 2
    inv_freq = jnp.asarray(ROPE_THETA, F32) ** (-jnp.arange(0, axis_dim, 2, dtype=F32) / axis_dim)
    ang_r = jnp.asarray(row_idx)[:, None] * inv_freq
    ang_c = jnp.asarray(col_idx)[:, None] * inv_freq
    zeros = jnp.zeros_like(ang_r)
    cos = jnp.concatenate([jnp.cos(ang_r)] * 2 + [jnp.cos(ang_c)] * 2, axis=1)
    s_lo = jnp.concatenate([-jnp.sin(ang_r), zeros, -jnp.sin(ang_c), zeros], axis=1)
    s_hi = jnp.concatenate([zeros, jnp.sin(ang_r), zeros, jnp.sin(ang_c)], axis=1)
    return cos, s_lo, s_hi


def _qkv_body(x_ref, w_ref, gp_ref, c_ref, slo_ref, shi_ref, o_ref, *, n_qk_tiles, heads_per_tile):
    j = pl.program_id(1)
    acc = jnp.dot(x_ref[...], w_ref[...], preferred_element_type=F32)

    @pl.when(j < n_qk_tiles)
    def _():
        gain = gp_ref[0:1, :]
        post = gp_ref[1:2, :]
        cos, s_lo, s_hi = c_ref[...], slo_ref[...], shi_ref[...]
        for h in range(heads_per_tile):
            sl = slice(h * HEAD_DIM, (h + 1) * HEAD_DIM)
            y = _rms(acc[:, sl], gain)
            r = (y * cos + pltpu.roll(y, HEAD_DIM - 32, 1) * s_lo + pltpu.roll(y, 32, 1) * s_hi)
            o_ref[:, sl] = (r * post).astype(o_ref.dtype)

    @pl.when(j >= n_qk_tiles)
    def _():
        o_ref[...] = acc.astype(o_ref.dtype)


def qkv_projection(hn, w_qkv, q_gain, k_gain, seq_len):
    t, d = hn.shape
    n = w_qkv.shape[1]
    kv_cols = d // GQA_GROUP
    tm = _tile(seq_len, 1024)
    tn = _tile(kv_cols, 1024)
    n_q_tiles, n_k_tiles = d // tn, kv_cols // tn
    scale = HEAD_DIM ** -0.5 * np.log2(np.e)
    ones = jnp.ones((HEAD_DIM,), F32)
    rows = ([jnp.stack([q_gain, ones * scale])] * n_q_tiles + [jnp.stack([k_gain, ones])] * n_k_tiles
            + [jnp.stack([ones, ones])] * n_k_tiles)
    gp = jnp.stack(rows)
    cos, s_lo, s_hi = _rope_tables(seq_len)
    seq_tiles = seq_len // tm
    tab = pl.BlockSpec((tm, HEAD_DIM), lambda i, j: (i % seq_tiles, 0))
    body = functools.partial(_qkv_body, n_qk_tiles=n_q_tiles + n_k_tiles, heads_per_tile=tn // HEAD_DIM)
    return pl.pallas_call(
        body,
        grid=(t // tm, n // tn),
        in_specs=[
            pl.BlockSpec((tm, d), lambda i, j: (i, 0)),
            pl.BlockSpec((d, tn), lambda i, j: (0, j)),
            pl.BlockSpec((None, 2, HEAD_DIM), lambda i, j: (j, 0, 0)),
            tab, tab, tab,
        ],
        out_specs=pl.BlockSpec((tm, tn), lambda i, j: (i, j)),
        out_shape=jax.ShapeDtypeStruct((t, n), BF16),
        compiler_params=_params("parallel", "arbitrary"),
        name="qkv_projection",
    )(hn, w_qkv, gp, cos, s_lo, s_hi)


def _attention_body(q_ref, k_ref, v_ref, o_ref, vx_ref, *, rows_per_chain):
    @pl.when(pl.program_id(2) == 0)
    def _():
        vx_ref[:, 0:HEAD_DIM] = v_ref[...]
        vx_ref[:, HEAD_DIM:2 * HEAD_DIM] = jnp.ones(v_ref.shape, BF16)

    k = k_ref[...]
    for r0 in range(0, q_ref.shape[0], rows_per_chain):
        rows = slice(r0, r0 + rows_per_chain)
        for h in range(GQA_GROUP):
            sl = slice(h * HEAD_DIM, (h + 1) * HEAD_DIM)
            s = lax.dot_general(q_ref[rows, sl], k, (((1,), (1,)), ((), ())), preferred_element_type=F32)
            m = jnp.max(s, axis=-1, keepdims=True)
            p = jnp.exp2(s - m).astype(BF16)
            o = jnp.dot(p, vx_ref[...], preferred_element_type=F32)
            o_ref[rows, sl] = (o[:, 0:HEAD_DIM] / o[:, HEAD_DIM:2 * HEAD_DIM]).astype(o_ref.dtype)


def attention(qkv, batch, seq_len, d):
    t = qkv.shape[0]
    n_q = d // HEAD_DIM
    n_kv = n_q // GQA_GROUP
    tq = _tile(seq_len, 512)
    q_tiles = seq_len // tq
    gw = GQA_GROUP * HEAD_DIM
    body = functools.partial(_attention_body, rows_per_chain=_tile(tq, 128))
    return pl.pallas_call(
        body,
        grid=(batch, n_kv, q_tiles),
        in_specs=[
            pl.BlockSpec((tq, gw), lambda b, g, i: (b * q_tiles + i, g)),
            pl.BlockSpec((seq_len, HEAD_DIM), lambda b, g, i: (b, n_q + g)),
            pl.BlockSpec((seq_len, HEAD_DIM), lambda b, g, i: (b, n_q + n_kv + g)),
        ],
        out_specs=pl.BlockSpec((tq, gw), lambda b, g, i: (b * q_tiles + i, g)),
        out_shape=jax.ShapeDtypeStruct((t, d), BF16),
        scratch_shapes=[pltpu.VMEM((seq_len, 2 * HEAD_DIM), BF16)],
        compiler_params=_params("parallel", "parallel", "arbitrary"),
        name="attention",
    )(qkv, qkv, qkv)


def _proj_residual_body(x_ref, w_ref, r_ref, o_ref):
    o_ref[...] = r_ref[...] + jnp.dot(x_ref[...], w_ref[...], preferred_element_type=F32)


def proj_residual(x, w, res):
    t, k = x.shape
    n = w.shape[1]
    tm = _tile(t, 512)
    tn = _tile(n, 1024)
    return pl.pallas_call(
        _proj_residual_body,
        grid=(t // tm, n // tn),
        in_specs=[
            pl.BlockSpec((tm, k), lambda i, j: (i, 0)),
            pl.BlockSpec((k, tn), lambda i, j: (0, j)),
            pl.BlockSpec((tm, tn), lambda i, j: (i, j)),
        ],
        out_specs=pl.BlockSpec((tm, tn), lambda i, j: (i, j)),
        out_shape=jax.ShapeDtypeStruct((t, n), F32),
        compiler_params=_params("parallel", "arbitrary"),
        name="proj_residual",
    )(x, w, res)


def _conv_in_body(x_ref, wb_ref, wc_ref, wx_ref, b_ref, u_ref):
    x = x_ref[...]
    b_ref[...] = jnp.dot(x, wb_ref[...], preferred_element_type=F32).astype(b_ref.dtype)
    c = jnp.dot(x, wc_ref[...], preferred_element_type=F32)
    xin = jnp.dot(x, wx_ref[...], preferred_element_type=F32)
    u_ref[...] = (c * xin).astype(u_ref.dtype)


def conv_in_projection(hn, w_in):
    t, d = hn.shape
    c = w_in.shape[1] // 3
    tm = _tile(t, 512)
    tn = _tile(c, 512)
    nc = c // tn
    out = pl.BlockSpec((tm, tn), lambda i, j: (i, j))
    return pl.pallas_call(
        _conv_in_body,
        grid=(t // tm, nc),
        in_specs=[
            pl.BlockSpec((tm, d), lambda i, j: (i, 0)),
            pl.BlockSpec((d, tn), lambda i, j: (0, j)),
            pl.BlockSpec((d, tn), lambda i, j: (0, nc + j)),
            pl.BlockSpec((d, tn), lambda i, j: (0, 2 * nc + j)),
        ],
        out_specs=[out, out],
        out_shape=[jax.ShapeDtypeStruct((t, c), BF16)] * 2,
        compiler_params=_params("parallel", "arbitrary"),
        name="conv_in_projection",
    )(hn, w_in, w_in, w_in)


def _conv_body(b_ref, u_ref, w_ref, bias_ref, o_ref):
    u = u_ref[...].astype(F32)
    s = u.shape[0]
    pos = lax.broadcasted_iota(jnp.int32, u.shape, 0)
    prev = jnp.where(pos == 0, 0.0, pltpu.roll(u, 1, 0))
    nxt = jnp.where(pos == s - 1, 0.0, pltpu.roll(u, s - 1, 0))
    conv = prev * w_ref[0:1, :] + u * w_ref[1:2, :] + nxt * w_ref[2:3, :] + bias_ref[...]
    o_ref[...] = (b_ref[...].astype(F32) * conv).astype(o_ref.dtype)


def gated_conv(b, u, conv_w, conv_b, seq_len):
    t, c = u.shape
    tc = _tile(c, 256)
    blk = pl.BlockSpec((seq_len, tc), lambda i, j: (i, j))
    return pl.pallas_call(
        _conv_body,
        grid=(t // seq_len, c // tc),
        in_specs=[blk, blk, pl.BlockSpec((3, tc), lambda i, j: (0, j)), pl.BlockSpec((1, tc), lambda i, j: (0, j))],
        out_specs=blk,
        out_shape=jax.ShapeDtypeStruct((t, c), BF16),
        compiler_params=_params("parallel", "parallel"),
        name="gated_conv",
    )(b, u, conv_w, conv_b.reshape(1, c))


def _query_t_body(w_ref, x_ref, o_ref):
    o_ref[...] = jnp.dot(w_ref[...], x_ref[...], preferred_element_type=F32)


def peer_query_t(wq_t, hn_t):
    q, d = wq_t.shape
    t = hn_t.shape[1]
    tm = _tile(q, 1024)
    tn = _tile(t, 512)
    return pl.pallas_call(
        _query_t_body,
        grid=(q // tm, t // tn),
        in_specs=[pl.BlockSpec((tm, d), lambda i, j: (i, 0)), pl.BlockSpec((d, tn), lambda i, j: (0, j))],
        out_specs=pl.BlockSpec((tm, tn), lambda i, j: (i, j)),
        out_shape=jax.ShapeDtypeStruct((q, t), F32),
        compiler_params=_params("parallel", "arbitrary"),
        name="peer_query",
    )(wq_t, hn_t)


def _extract16(x, break_ties):
    n_rows = x.shape[0]
    iota = lax.broadcasted_iota(jnp.int32, x.shape, 0).astype(F32) if break_ties else None
    work = x
    rank = jnp.full(x.shape, float(PEER_TOPK), F32)
    vals = []
    for p in range(PEER_TOPK):
        m = jnp.max(work, axis=0, keepdims=True)
        hit = work == m
        if break_ties:
            hit = iota == jnp.min(jnp.where(hit, iota, float(n_rows)), axis=0, keepdims=True)
        rank = jnp.where(hit, float(p), rank)
        work = jnp.where(hit, -jnp.inf, work)
        vals.append(m)
    count = jnp.sum(jnp.where(rank < float(PEER_TOPK), 1.0, 0.0), axis=0, keepdims=True)
    return vals, rank, count


def _select_tile(s1, s2, outs, scratch, break_ties):
    rank2_ref, lr_ref, e1_ref, e2_ref = outs
    cand_ref, sel_ref = scratch
    v1, rank1, n1 = _extract16(s1, break_ties)
    v2, rank2, n2 = _extract16(s2, break_ties)

    cand_ref[...] = jnp.full(cand_ref.shape, -jnp.inf, F32)
    for r, (p1, p2) in enumerate(_CAND):
        cand_ref[r:r + 1, :] = v1[p1] + v2[p2]
    cand = cand_ref[...]
    _, crank, n3 = _extract16(cand, break_ties)
    sel = jnp.where(crank < float(PEER_TOPK), 1.0, 0.0)
    z = jnp.sum(sel * jnp.exp(cand - cand[0:1, :]), axis=0, keepdims=True)
    sel_ref[...] = sel

    lr = jnp.zeros(s1.shape, F32)
    r0 = 0
    for p1 in range(PEER_TOPK):
        n = PEER_TOPK // (p1 + 1)
        count = jnp.sum(sel_ref[r0:r0 + n, :], axis=0, keepdims=True)
        lr = jnp.where(rank1 == float(p1), count, lr)
        r0 += n

    lr_ref[...] = lr
    e1_ref[...] = jnp.exp(s1 - v1[0])
    rank2 = rank2.astype(BF16)
    e2 = (jnp.exp(s2 - v2[0]) / z).astype(BF16)
    for r in range(N_KEYS // GATE_ROWS):
        rows = slice(r * GATE_ROWS, (r + 1) * GATE_ROWS)
        rank2_ref[r] = rank2[rows, :]
        e2_ref[r] = e2[rows, :]
    full = float(PEER_TOPK)
    return jnp.where(n1 == full, 1.0, 0.0) + jnp.where(n2 == full, 1.0, 0.0) + jnp.where(n3 == full, 1.0, 0.0)


def _select_body(qt_ref, keys_ref, rank2_ref, lr_ref, e1_ref, e2_ref, cand_ref, sel_ref):
    qt = qt_ref[...].astype(BF16)
    s1 = jnp.dot(keys_ref[0].astype(BF16), qt[:PEER_HALF], preferred_element_type=F32)
    s2 = jnp.dot(keys_ref[1].astype(BF16), qt[PEER_HALF:], preferred_element_type=F32)
    outs = (rank2_ref, lr_ref, e1_ref, e2_ref)
    scratch = (cand_ref, sel_ref)
    clean = _select_tile(s1, s2, outs, scratch, break_ties=False)

    @pl.when(jnp.min(clean) < 3.0)
    def _():
        _select_tile(s1, s2, outs, scratch, break_ties=True)


def peer_select(q_t, sub_keys):
    t = q_t.shape[1]
    tb = GATE_LANES
    assert t % tb == 0
    n_r = N_KEYS // GATE_ROWS
    row_out = pl.BlockSpec((None, N_KEYS, tb), lambda h, i: (h, 0, i))
    row_shape = jax.ShapeDtypeStruct((PEER_HEADS, N_KEYS, t), F32)
    tile_out = pl.BlockSpec((None, n_r, None, GATE_ROWS, tb), lambda h, i: (i, 0, h, 0, 0))
    tile_shape = jax.ShapeDtypeStruct((t // tb, n_r, PEER_HEADS, GATE_ROWS, tb), BF16)
    return pl.pallas_call(
        _select_body,
        grid=(PEER_HEADS, t // tb),
        in_specs=[
            pl.BlockSpec((2 * PEER_HALF, tb), lambda h, i: (h, i)),
            pl.BlockSpec((None, 2, N_KEYS, PEER_HALF), lambda h, i: (h, 0, 0, 0)),
        ],
        out_specs=[tile_out, row_out, row_out, tile_out],
        out_shape=[tile_shape, row_shape, row_shape, tile_shape],
        scratch_shapes=[pltpu.VMEM((_CAND_ROWS, tb), F32), pltpu.VMEM((_CAND_ROWS, tb), F32)],
        compiler_params=_params("parallel", "arbitrary"),
        name="peer_select",
    )(q_t, sub_keys)


def _gelu(a):
    return 0.5 * a * (1.0 + lax.erf(a * np.float32(np.sqrt(0.5))))


def _gate_tile(rank2_ref, e2_ref, lr_ref, e1_ref, g_ref, row0):
    zero = jnp.zeros((), BF16)
    piece = (GATE_ROWS, GATE_LANES)
    for c in range(g_ref.shape[1] // GATE_LANES):
        cols = slice(c * GATE_LANES, (c + 1) * GATE_LANES)
        for j in range(g_ref.shape[0] // N_KEYS):
            row = pl.ds(row0 + j, 1)
            lrows = [jnp.broadcast_to(lr_ref[h, row, cols], piece).astype(BF16) for h in range(PEER_HEADS)]
            erows = [jnp.broadcast_to(e1_ref[h, row, cols], piece).astype(BF16) for h in range(PEER_HEADS)]
            for r in range(N_KEYS // GATE_ROWS):
                gate = None
                for h in range(PEER_HEADS):
                    term = jnp.where(rank2_ref[c, r, h] < lrows[h], e2_ref[c, r, h] * erows[h], zero)
                    gate = term if gate is None else gate + term
                g_ref[j * N_KEYS + r * GATE_ROWS:j * N_KEYS + (r + 1) * GATE_ROWS, cols] = gate


def _peer_dense_body(xn_ref, u_ref, vt_ref, rank2_ref, e2_ref, lr_ref, e1_ref, o_ref, g_ref, pt_ref):
    @pl.when(pl.program_id(1) == 0)
    def _():
        o_ref[...] = jnp.zeros(o_ref.shape, F32)

    n_sub = g_ref.shape[0] // N_KEYS
    tiles_per_block = lr_ref.shape[1] // n_sub
    row0 = (pl.program_id(1) % tiles_per_block) * n_sub
    _gate_tile(rank2_ref, e2_ref, lr_ref, e1_ref, g_ref, row0)
    a_t = jnp.dot(u_ref[...], xn_ref[...], preferred_element_type=F32)
    pt_ref[...] = g_ref[...] * _gelu(a_t).astype(BF16)
    o_ref[...] += jnp.dot(vt_ref[...], pt_ref[...], preferred_element_type=F32)


def peer_dense_t(hn_t, u, v_t, rank2, lr, e1, e2):
    d, t = hn_t.shape
    n_exp = u.shape[0]
    tb = _tile(t, 512)
    eb = _tile(n_exp, 512)
    n_sub = eb // N_KEYS
    sublanes = 8
    assert sublanes % n_sub == 0
    tiles_per_block = sublanes // n_sub
    once = pl.Buffered(1)
    tok = pl.BlockSpec((tb // GATE_LANES,) + rank2.shape[1:], lambda i, e: (i, 0, 0, 0, 0), pipeline_mode=once)
    per_tile = pl.BlockSpec((PEER_HEADS, sublanes, tb), lambda i, e: (0, e // tiles_per_block, i))
    return pl.pallas_call(
        _peer_dense_body,
        grid=(t // tb, n_exp // eb),
        in_specs=[
            pl.BlockSpec((d, tb), lambda i, e: (0, i), pipeline_mode=once),
            pl.BlockSpec((eb, d), lambda i, e: (e, 0)),
            pl.BlockSpec((d, eb), lambda i, e: (0, e)),
            tok, tok, per_tile, per_tile,
        ],
        out_specs=pl.BlockSpec((d, tb), lambda i, e: (0, i)),
        out_shape=jax.ShapeDtypeStruct((d, t), F32),
        scratch_shapes=[pltpu.VMEM((eb, tb), BF16), pltpu.VMEM((eb, tb), BF16)],
        compiler_params=_params("parallel", "arbitrary"),
        name="peer_dense",
    )(hn_t, u, v_t, rank2, e2, lr, e1)


def _cast_body(x_ref, o_ref):
    o_ref[...] = x_ref[...].astype(o_ref.dtype)


def _cast_t_body(x_ref, o_ref):
    o_ref[...] = x_ref[...].T.astype(o_ref.dtype)


def cast_bf16(x, transpose=False):
    r, c = x.shape
    tr = _tile(r, 512)
    out_spec = pl.BlockSpec((c, tr), lambda i: (0, i)) if transpose else pl.BlockSpec((tr, c), lambda i: (i, 0))
    return pl.pallas_call(
        _cast_t_body if transpose else _cast_body,
        grid=(r // tr,),
        in_specs=[pl.BlockSpec((tr, c), lambda i: (i, 0))],
        out_specs=out_spec,
        out_shape=jax.ShapeDtypeStruct((c, r) if transpose else (r, c), BF16),
        compiler_params=_params("parallel"),
        name="cast_t_bf16" if transpose else "cast_bf16",
    )(x)


def peer_ffn_t(hn_t, w_query, sub_keys, expert_u, expert_v):
    q_t = peer_query_t(w_query.T.astype(BF16), hn_t)
    rank2, lr, e1, e2 = peer_select(q_t, sub_keys)
    return peer_dense_t(hn_t, cast_bf16(expert_u), cast_bf16(expert_v, transpose=True), rank2, lr, e1, e2)


def kernel(x, mixer_norm_g, ffn_norm_g, attn_w_qkv, attn_w_o, attn_q_gain, attn_k_gain, conv_w_in, conv_w, conv_b,
           conv_w_out, peer_w_query, peer_sub_keys, peer_u, peer_v):
    batch, seq_len, d = x.shape
    h = x.reshape(batch * seq_len, d)
    depth = mixer_norm_g.shape[0]
    hn = rmsnorm(h, mixer_norm_g[0])
    for layer in range(depth):
        j = layer // 2
        if layer % 2 == 0:
            qkv = qkv_projection(hn, attn_w_qkv[j].astype(BF16), attn_q_gain[j], attn_k_gain[j], seq_len)
            o = attention(qkv, batch, seq_len, d)
            h = proj_residual(o, attn_w_o[j].astype(BF16), h)
        else:
            b, u = conv_in_projection(hn, conv_w_in[j].astype(BF16))
            y = gated_conv(b, u, conv_w[j], conv_b[j], seq_len)
            h = proj_residual(y, conv_w_out[j].astype(BF16), h)
        hn_t = rmsnorm_t(h, ffn_norm_g[layer])
        ffn_t = peer_ffn_t(hn_t, peer_w_query[layer], peer_sub_keys[layer], peer_u[layer], peer_v[layer])
        if layer + 1 < depth:
            h, hn = add_rmsnorm(h, ffn_t, mixer_norm_g[layer + 1])
        else:
            h = add(h, ffn_t)
    return h.reshape(batch, seq_len, d)
```

```python
import functools

import numpy as np
import jax
import jax.numpy as jnp
from jax import lax
from jax.experimental import pallas as pl
from jax.experimental.pallas import tpu as pltpu

HEAD_DIM = 128
GQA_GROUP = 4
ROPE_THETA = 10000.0
GRID_W = 64
PEER_HEADS = 8
N_KEYS = 128
PEER_HALF = 128
PEER_TOPK = 16
EPS = 1e-6

LANES = 128
GATE_ROWS = 16
GATE_LANES = 2 * LANES
VMEM_LIMIT_BYTES = 56 * 2**20
F32 = jnp.float32
BF16 = jnp.bfloat16

_CAND = [(p1, p2) for p1 in range(PEER_TOPK) for p2 in range(PEER_TOPK // (p1 + 1))]
_CAND_ROWS = -(-len(_CAND) // 8) * 8


def _params(*semantics):
    return pltpu.CompilerParams(dimension_semantics=semantics, vmem_limit_bytes=VMEM_LIMIT_BYTES)


def _tile(n, pref):
    t = min(n, pref)
    assert n % t == 0, (n, pref)
    return t


def _rms(x, g):
    ms = jnp.mean(x * x, axis=-1, keepdims=True)
    return x * lax.rsqrt(ms + EPS) * g


def _rmsnorm_body(x_ref, g_ref, o_ref):
    o_ref[...] = _rms(x_ref[...], g_ref[...]).astype(o_ref.dtype)


def _add_rmsnorm_body(a_ref, bt_ref, g_ref, h_ref, o_ref):
    h = a_ref[...] + bt_ref[...].T
    h_ref[...] = h
    o_ref[...] = _rms(h, g_ref[...]).astype(o_ref.dtype)


def _add_body(a_ref, bt_ref, o_ref):
    o_ref[...] = a_ref[...] + bt_ref[...].T


def rmsnorm(x, g):
    t, d = x.shape
    tr = _tile(t, 256)
    row = pl.BlockSpec((tr, d), lambda i: (i, 0))
    return pl.pallas_call(
        _rmsnorm_body,
        grid=(t // tr,),
        in_specs=[row, pl.BlockSpec((1, d), lambda i: (0, 0))],
        out_specs=row,
        out_shape=jax.ShapeDtypeStruct((t, d), BF16),
        compiler_params=_params("parallel"),
        name="rmsnorm",
    )(x, g.reshape(1, d))


def _rmsnorm_t_body(x_ref, g_ref, o_ref):
    o_ref[...] = _rms(x_ref[...], g_ref[...]).T.astype(o_ref.dtype)


def rmsnorm_t(x, g):
    t, d = x.shape
    tr = _tile(t, 256)
    return pl.pallas_call(
        _rmsnorm_t_body,
        grid=(t // tr,),
        in_specs=[pl.BlockSpec((tr, d), lambda i: (i, 0)), pl.BlockSpec((1, d), lambda i: (0, 0))],
        out_specs=pl.BlockSpec((d, tr), lambda i: (0, i)),
        out_shape=jax.ShapeDtypeStruct((d, t), BF16),
        compiler_params=_params("parallel"),
        name="rmsnorm_t",
    )(x, g.reshape(1, d))


def add_rmsnorm(a, b_t, g):
    t, d = a.shape
    tr = _tile(t, 256)
    row = pl.BlockSpec((tr, d), lambda i: (i, 0))
    col = pl.BlockSpec((d, tr), lambda i: (0, i))
    return pl.pallas_call(
        _add_rmsnorm_body,
        grid=(t // tr,),
        in_specs=[row, col, pl.BlockSpec((1, d), lambda i: (0, 0))],
        out_specs=[row, row],
        out_shape=[jax.ShapeDtypeStruct((t, d), F32), jax.ShapeDtypeStruct((t, d), BF16)],
        compiler_params=_params("parallel"),
        name="add_rmsnorm",
    )(a, b_t, g.reshape(1, d))


def add(a, b_t):
    t, d = a.shape
    tr = _tile(t, 256)
    row = pl.BlockSpec((tr, d), lambda i: (i, 0))
    return pl.pallas_call(
        _add_body,
        grid=(t // tr,),
        in_specs=[row, pl.BlockSpec((d, tr), lambda i: (0, i))],
        out_specs=row,
        out_shape=jax.ShapeDtypeStruct((t, d), F32),
        compiler_params=_params("parallel"),
        name="residual_add",
    )(a, b_t)


def _rope_tables(seq_len):
    pos = np.arange(seq_len)
    row_idx = (pos // GRID_W).astype(np.float32)
    col_idx = (pos % GRID_W).astype(np.float32)
    axis_dim = HEAD_DIM // 2
    inv_freq = jnp.asarray(ROPE_THETA, F32) ** (-jnp.arange(0, axis_dim, 2, dtype=F32) / axis_dim)
    ang_r = jnp.asarray(row_idx)[:, None] * inv_freq
    ang_c = jnp.asarray(col_idx)[:, None] * inv_freq
    zeros = jnp.zeros_like(ang_r)
    cos = jnp.concatenate([jnp.cos(ang_r)] * 2 + [jnp.cos(ang_c)] * 2, axis=1)
    s_lo = jnp.concatenate([-jnp.sin(ang_r), zeros, -jnp.sin(ang_c), zeros], axis=1)
    s_hi = jnp.concatenate([zeros, jnp.sin(ang_r), zeros, jnp.sin(ang_c)], axis=1)
    return cos, s_lo, s_hi


def _qkv_body(x_ref, w_ref, gp_ref, c_ref, slo_ref, shi_ref, o_ref, *, n_qk_tiles, heads_per_tile):
    j = pl.program_id(1)
    acc = jnp.dot(x_ref[...], w_ref[...], preferred_element_type=F32)

    @pl.when(j < n_qk_tiles)
    def _():
        gain = gp_ref[0:1, :]
        post = gp_ref[1:2, :]
        cos, s_lo, s_hi = c_ref[...], slo_ref[...], shi_ref[...]
        for h in range(heads_per_tile):
            sl = slice(h * HEAD_DIM, (h + 1) * HEAD_DIM)
            y = _rms(acc[:, sl], gain)
            r = (y * cos + pltpu.roll(y, HEAD_DIM - 32, 1) * s_lo + pltpu.roll(y, 32, 1) * s_hi)
            o_ref[:, sl] = (r * post).astype(o_ref.dtype)

    @pl.when(j >= n_qk_tiles)
    def _():
        o_ref[...] = acc.astype(o_ref.dtype)


def qkv_projection(hn, w_qkv, q_gain, k_gain, seq_len):
    t, d = hn.shape
    n = w_qkv.shape[1]
    kv_cols = d // GQA_GROUP
    tm = _tile(seq_len, 1024)
    tn = _tile(kv_cols, 1024)
    n_q_tiles, n_k_tiles = d // tn, kv_cols // tn
    scale = HEAD_DIM ** -0.5 * np.log2(np.e)
    ones = jnp.ones((HEAD_DIM,), F32)
    rows = ([jnp.stack([q_gain, ones * scale])] * n_q_tiles + [jnp.stack([k_gain, ones])] * n_k_tiles
            + [jnp.stack([ones, ones])] * n_k_tiles)
    gp = jnp.stack(rows)
    cos, s_lo, s_hi = _rope_tables(seq_len)
    seq_tiles = seq_len // tm
    tab = pl.BlockSpec((tm, HEAD_DIM), lambda i, j: (i % seq_tiles, 0))
    body = functools.partial(_qkv_body, n_qk_tiles=n_q_tiles + n_k_tiles, heads_per_tile=tn // HEAD_DIM)
    return pl.pallas_call(
        body,
        grid=(t // tm, n // tn),
        in_specs=[
            pl.BlockSpec((tm, d), lambda i, j: (i, 0)),
            pl.BlockSpec((d, tn), lambda i, j: (0, j)),
            pl.BlockSpec((None, 2, HEAD_DIM), lambda i, j: (j, 0, 0)),
            tab, tab, tab,
        ],
        out_specs=pl.BlockSpec((tm, tn), lambda i, j: (i, j)),
        out_shape=jax.ShapeDtypeStruct((t, n), BF16),
        compiler_params=_params("parallel", "arbitrary"),
        name="qkv_projection",
    )(hn, w_qkv, gp, cos, s_lo, s_hi)


def _attention_body(q_ref, k_ref, v_ref, o_ref, vx_ref, *, rows_per_chain):
    @pl.when(pl.program_id(2) == 0)
    def _():
        vx_ref[:, 0:HEAD_DIM] = v_ref[...]
        vx_ref[:, HEAD_DIM:2 * HEAD_DIM] = jnp.ones(v_ref.shape, BF16)

    k = k_ref[...]
    for r0 in range(0, q_ref.shape[0], rows_per_chain):
        rows = slice(r0, r0 + rows_per_chain)
        for h in range(GQA_GROUP):
            sl = slice(h * HEAD_DIM, (h + 1) * HEAD_DIM)
            s = lax.dot_general(q_ref[rows, sl], k, (((1,), (1,)), ((), ())), preferred_element_type=F32)
            m = jnp.max(s, axis=-1, keepdims=True)
            p = jnp.exp2(s - m).astype(BF16)
            o = jnp.dot(p, vx_ref[...], preferred_element_type=F32)
            o_ref[rows, sl] = (o[:, 0:HEAD_DIM] / o[:, HEAD_DIM:2 * HEAD_DIM]).astype(o_ref.dtype)


def attention(qkv, batch, seq_len, d):
    t = qkv.shape[0]
    n_q = d // HEAD_DIM
    n_kv = n_q // GQA_GROUP
    tq = _tile(seq_len, 512)
    q_tiles = seq_len // tq
    gw = GQA_GROUP * HEAD_DIM
    body = functools.partial(_attention_body, rows_per_chain=_tile(tq, 128))
    return pl.pallas_call(
        body,
        grid=(batch, n_kv, q_tiles),
        in_specs=[
            pl.BlockSpec((tq, gw), lambda b, g, i: (b * q_tiles + i, g)),
            pl.BlockSpec((seq_len, HEAD_DIM), lambda b, g, i: (b, n_q + g)),
            pl.BlockSpec((seq_len, HEAD_DIM), lambda b, g, i: (b, n_q + n_kv + g)),
        ],
        out_specs=pl.BlockSpec((tq, gw), lambda b, g, i: (b * q_tiles + i, g)),
        out_shape=jax.ShapeDtypeStruct((t, d), BF16),
        scratch_shapes=[pltpu.VMEM((seq_len, 2 * HEAD_DIM), BF16)],
        compiler_params=_params("parallel", "parallel", "arbitrary"),
        name="attention",
    )(qkv, qkv, qkv)


def _proj_residual_body(x_ref, w_ref, r_ref, o_ref):
    o_ref[...] = r_ref[...] + jnp.dot(x_ref[...], w_ref[...], preferred_element_type=F32)


def proj_residual(x, w, res):
    t, k = x.shape
    n = w.shape[1]
    tm = _tile(t, 512)
    tn = _tile(n, 1024)
    return pl.pallas_call(
        _proj_residual_body,
        grid=(n // tn, t // tm),
        in_specs=[
            pl.BlockSpec((tm, k), lambda j, i: (i, 0)),
            pl.BlockSpec((k, tn), lambda j, i: (0, j)),
            pl.BlockSpec((tm, tn), lambda j, i: (i, j)),
        ],
        out_specs=pl.BlockSpec((tm, tn), lambda j, i: (i, j)),
        out_shape=jax.ShapeDtypeStruct((t, n), F32),
        compiler_params=_params("parallel", "arbitrary"),
        name="proj_residual",
    )(x, w, res)


def _conv_in_body(x_ref, wb_ref, wc_ref, wx_ref, b_ref, u_ref):
    x = x_ref[...]
    b_ref[...] = jnp.dot(x, wb_ref[...], preferred_element_type=F32).astype(b_ref.dtype)
    c = jnp.dot(x, wc_ref[...], preferred_element_type=F32)
    xin = jnp.dot(x, wx_ref[...], preferred_element_type=F32)
    u_ref[...] = (c * xin).astype(u_ref.dtype)


def conv_in_projection(hn, w_in):
    t, d = hn.shape
    c = w_in.shape[1] // 3
    tm = _tile(t, 512)
    tn = _tile(c, 512)
    nc = c // tn
    out = pl.BlockSpec((tm, tn), lambda i, j: (i, j))
    return pl.pallas_call(
        _conv_in_body,
        grid=(t // tm, nc),
        in_specs=[
            pl.BlockSpec((tm, d), lambda i, j: (i, 0)),
            pl.BlockSpec((d, tn), lambda i, j: (0, j)),
            pl.BlockSpec((d, tn), lambda i, j: (0, nc + j)),
            pl.BlockSpec((d, tn), lambda i, j: (0, 2 * nc + j)),
        ],
        out_specs=[out, out],
        out_shape=[jax.ShapeDtypeStruct((t, c), BF16)] * 2,
        compiler_params=_params("parallel", "arbitrary"),
        name="conv_in_projection",
    )(hn, w_in, w_in, w_in)


def _conv_body(b_ref, u_ref, w_ref, bias_ref, o_ref):
    u = u_ref[...].astype(F32)
    s = u.shape[0]
    pos = lax.broadcasted_iota(jnp.int32, u.shape, 0)
    prev = jnp.where(pos == 0, 0.0, pltpu.roll(u, 1, 0))
    nxt = jnp.where(pos == s - 1, 0.0, pltpu.roll(u, s - 1, 0))
    conv = prev * w_ref[0:1, :] + u * w_ref[1:2, :] + nxt * w_ref[2:3, :] + bias_ref[...]
    o_ref[...] = (b_ref[...].astype(F32) * conv).astype(o_ref.dtype)


def gated_conv(b, u, conv_w, conv_b, seq_len):
    t, c = u.shape
    tc = _tile(c, 256)
    blk = pl.BlockSpec((seq_len, tc), lambda i, j: (i, j))
    return pl.pallas_call(
        _conv_body,
        grid=(t // seq_len, c // tc),
        in_specs=[blk, blk, pl.BlockSpec((3, tc), lambda i, j: (0, j)), pl.BlockSpec((1, tc), lambda i, j: (0, j))],
        out_specs=blk,
        out_shape=jax.ShapeDtypeStruct((t, c), BF16),
        compiler_params=_params("parallel", "parallel"),
        name="gated_conv",
    )(b, u, conv_w, conv_b.reshape(1, c))


def _query_t_body(w_ref, x_ref, o_ref):
    o_ref[...] = jnp.dot(w_ref[...], x_ref[...], preferred_element_type=F32)


def peer_query_t(wq_t, hn_t):
    q, d = wq_t.shape
    t = hn_t.shape[1]
    tm = _tile(q, 1024)
    tn = _tile(t, 512)
    return pl.pallas_call(
        _query_t_body,
        grid=(q // tm, t // tn),
        in_specs=[pl.BlockSpec((tm, d), lambda i, j: (i, 0)), pl.BlockSpec((d, tn), lambda i, j: (0, j))],
        out_specs=pl.BlockSpec((tm, tn), lambda i, j: (i, j)),
        out_shape=jax.ShapeDtypeStruct((q, t), F32),
        compiler_params=_params("parallel", "arbitrary"),
        name="peer_query",
    )(wq_t, hn_t)


def _extract16(x, break_ties):
    n_rows = x.shape[0]
    iota = lax.broadcasted_iota(jnp.int32, x.shape, 0).astype(F32) if break_ties else None
    work = x
    rank = jnp.full(x.shape, float(PEER_TOPK), F32)
    vals = []
    for p in range(PEER_TOPK):
        m = jnp.max(work, axis=0, keepdims=True)
        hit = work == m
        if break_ties:
            hit = iota == jnp.min(jnp.where(hit, iota, float(n_rows)), axis=0, keepdims=True)
        rank = jnp.where(hit, float(p), rank)
        work = jnp.where(hit, -jnp.inf, work)
        vals.append(m)
    count = jnp.sum(jnp.where(rank < float(PEER_TOPK), 1.0, 0.0), axis=0, keepdims=True)
    return vals, rank, count


def _select_tile(s1, s2, outs, scratch, break_ties):
    rank2_ref, lr_ref, e1_ref, e2_ref = outs
    cand_ref, sel_ref = scratch
    v1, rank1, n1 = _extract16(s1, break_ties)
    v2, rank2, n2 = _extract16(s2, break_ties)

    cand_ref[...] = jnp.full(cand_ref.shape, -jnp.inf, F32)
    for r, (p1, p2) in enumerate(_CAND):
        cand_ref[r:r + 1, :] = v1[p1] + v2[p2]
    cand = cand_ref[...]
    _, crank, n3 = _extract16(cand, break_ties)
    sel = jnp.where(crank < float(PEER_TOPK), 1.0, 0.0)
    z = jnp.sum(sel * jnp.exp(cand - cand[0:1, :]), axis=0, keepdims=True)
    sel_ref[...] = sel

    lr = jnp.zeros(s1.shape, F32)
    r0 = 0
    for p1 in range(PEER_TOPK):
        n = PEER_TOPK // (p1 + 1)
        count = jnp.sum(sel_ref[r0:r0 + n, :], axis=0, keepdims=True)
        lr = jnp.where(rank1 == float(p1), count, lr)
        r0 += n

    lr_ref[...] = lr
    e1_ref[...] = jnp.exp(s1 - v1[0])
    rank2 = rank2.astype(BF16)
    e2 = (jnp.exp(s2 - v2[0]) / z).astype(BF16)
    for r in range(N_KEYS // GATE_ROWS):
        rows = slice(r * GATE_ROWS, (r + 1) * GATE_ROWS)
        rank2_ref[r] = rank2[rows, :]
        e2_ref[r] = e2[rows, :]
    full = float(PEER_TOPK)
    return jnp.where(n1 == full, 1.0, 0.0) + jnp.where(n2 == full, 1.0, 0.0) + jnp.where(n3 == full, 1.0, 0.0)


def _select_body(qt_ref, keys_ref, rank2_ref, lr_ref, e1_ref, e2_ref, cand_ref, sel_ref):
    qt = qt_ref[...].astype(BF16)
    s1 = jnp.dot(keys_ref[0].astype(BF16), qt[:PEER_HALF], preferred_element_type=F32)
    s2 = jnp.dot(keys_ref[1].astype(BF16), qt[PEER_HALF:], preferred_element_type=F32)
    outs = (rank2_ref, lr_ref, e1_ref, e2_ref)
    scratch = (cand_ref, sel_ref)
    clean = _select_tile(s1, s2, outs, scratch, break_ties=False)

    @pl.when(jnp.min(clean) < 3.0)
    def _():
        _select_tile(s1, s2, outs, scratch, break_ties=True)


def peer_select(q_t, sub_keys):
    t = q_t.shape[1]
    tb = GATE_LANES
    assert t % tb == 0
    n_r = N_KEYS // GATE_ROWS
    row_out = pl.BlockSpec((None, N_KEYS, tb), lambda h, i: (h, 0, i))
    row_shape = jax.ShapeDtypeStruct((PEER_HEADS, N_KEYS, t), F32)
    tile_out = pl.BlockSpec((None, n_r, None, GATE_ROWS, tb), lambda h, i: (i, 0, h, 0, 0))
    tile_shape = jax.ShapeDtypeStruct((t // tb, n_r, PEER_HEADS, GATE_ROWS, tb), BF16)
    return pl.pallas_call(
        _select_body,
        grid=(PEER_HEADS, t // tb),
        in_specs=[
            pl.BlockSpec((2 * PEER_HALF, tb), lambda h, i: (h, i)),
            pl.BlockSpec((None, 2, N_KEYS, PEER_HALF), lambda h, i: (h, 0, 0, 0)),
        ],
        out_specs=[tile_out, row_out, row_out, tile_out],
        out_shape=[tile_shape, row_shape, row_shape, tile_shape],
        scratch_shapes=[pltpu.VMEM((_CAND_ROWS, tb), F32), pltpu.VMEM((_CAND_ROWS, tb), F32)],
        compiler_params=_params("parallel", "arbitrary"),
        name="peer_select",
    )(q_t, sub_keys)


def _gelu(a):
    return 0.5 * a * (1.0 + lax.erf(a * np.float32(np.sqrt(0.5))))


def _gate_tile(rank2_ref, e2_ref, lr_ref, e1_ref, g_ref, row0, anchor=None):
    zero = jnp.zeros((), BF16)
    piece = (GATE_ROWS, GATE_LANES)
    n_c, n_j = g_ref.shape[1] // GATE_LANES, g_ref.shape[0] // N_KEYS
    for c in range(n_c):
        cols = slice(c * GATE_LANES, (c + 1) * GATE_LANES)
        for j in range(n_j):
            row = pl.ds(row0 + j, 1)
            lrows = [lr_ref[h, row, cols] for h in range(PEER_HEADS)]
            if anchor is not None:
                a0 = (c * n_j + j) * (anchor.shape[0] // (n_c * n_j))
                bits = pltpu.bitcast(anchor[a0:a0 + 8, cols], jnp.uint32)
                zrow = pltpu.bitcast((bits >> 16) >> 16, F32)[0:1, :]
                lrows = [x + zrow for x in lrows]
            lrows = [jnp.broadcast_to(x, piece).astype(BF16) for x in lrows]
            erows = [jnp.broadcast_to(e1_ref[h, row, cols], piece).astype(BF16) for h in range(PEER_HEADS)]
            for r in range(N_KEYS // GATE_ROWS):
                gate = None
                for h in range(PEER_HEADS):
                    term = jnp.where(rank2_ref[c, r, h] < lrows[h], e2_ref[c, r, h] * erows[h], zero)
                    gate = term if gate is None else gate + term
                g_ref[j * N_KEYS + r * GATE_ROWS:j * N_KEYS + (r + 1) * GATE_ROWS, cols] = gate


def _peer_dense_body(xn_ref, u_ref, vt_ref, rank2_ref, e2_ref, lr0_ref, e10_ref, lrn_ref, e1n_ref, o_ref, g_ref,
                     pt_ref, *, n_tiles):
    e = pl.program_id(1)
    first_tile = e == 0

    @pl.when(first_tile)
    def _():
        o_ref[...] = jnp.zeros(o_ref.shape, F32)

    @pl.when(first_tile & (pl.program_id(0) == 0))
    def _():
        _gate_tile(rank2_ref, e2_ref, lr0_ref, e10_ref, g_ref, 0)

    n_sub = g_ref.shape[0] // N_KEYS
    tiles_per_block = lrn_ref.shape[1] // n_sub
    a_t = jnp.dot(u_ref[...], xn_ref[...], preferred_element_type=F32)
    pt_ref[...] = g_ref[...] * _gelu(a_t).astype(BF16)
    half = g_ref.shape[0] // 2
    part0 = jnp.dot(vt_ref[:, :half], pt_ref[:half, :], preferred_element_type=F32)
    part1 = jnp.dot(vt_ref[:, half:], pt_ref[half:, :], preferred_element_type=F32)
    o_ref[...] += part0 + part1
    row0 = (((e + 1) % n_tiles) % tiles_per_block) * n_sub
    _gate_tile(rank2_ref, e2_ref, lrn_ref, e1n_ref, g_ref, row0, anchor=part0)


def peer_dense_t(hn_t, u, v_t, rank2, lr, e1, e2):
    d, t = hn_t.shape
    n_exp = u.shape[0]
    tb = _tile(t, 512)
    eb = _tile(n_exp, 512)
    n_sub = eb // N_KEYS
    sublanes = 8
    assert sublanes % n_sub == 0
    tiles_per_block = sublanes // n_sub
    once = pl.Buffered(1)
    nt, ne = t // tb, n_exp // eb
    assert ne > 1

    def next_block(i, e):
        return jnp.minimum(i + (e + 1) // ne, nt - 1)

    tok = pl.BlockSpec((tb // GATE_LANES,) + rank2.shape[1:], lambda i, e: (next_block(i, e), 0, 0, 0, 0),
                       pipeline_mode=once)
    tile0 = pl.BlockSpec((PEER_HEADS, sublanes, tb), lambda i, e: (0, 0, 0))
    tile_next = pl.BlockSpec((PEER_HEADS, sublanes, tb),
                             lambda i, e: (0, ((e + 1) % ne) // tiles_per_block, next_block(i, e)))
    return pl.pallas_call(
        functools.partial(_peer_dense_body, n_tiles=ne),
        grid=(nt, ne),
        in_specs=[
            pl.BlockSpec((d, tb), lambda i, e: (0, i), pipeline_mode=once),
            pl.BlockSpec((eb, d), lambda i, e: (e, 0)),
            pl.BlockSpec((d, eb), lambda i, e: (0, e)),
            tok, tok, tile0, tile0, tile_next, tile_next,
        ],
        out_specs=pl.BlockSpec((d, tb), lambda i, e: (0, i)),
        out_shape=jax.ShapeDtypeStruct((d, t), F32),
        scratch_shapes=[pltpu.VMEM((eb, tb), BF16), pltpu.VMEM((eb, tb), BF16)],
        compiler_params=_params("arbitrary", "arbitrary"),
        name="peer_dense",
    )(hn_t, u, v_t, rank2, e2, lr, e1, lr, e1)


def _cast_body(x_ref, o_ref):
    o_ref[...] = x_ref[...].astype(o_ref.dtype)


def _cast_t_body(x_ref, o_ref):
    o_ref[...] = x_ref[...].T.astype(o_ref.dtype)


def cast_bf16(x, layer, transpose=False):
    _, r, c = x.shape
    tr = _tile(r, 512)
    out_spec = pl.BlockSpec((c, tr), lambda i: (0, i)) if transpose else pl.BlockSpec((tr, c), lambda i: (i, 0))
    return pl.pallas_call(
        _cast_t_body if transpose else _cast_body,
        grid=(r // tr,),
        in_specs=[pl.BlockSpec((None, tr, c), lambda i: (layer, i, 0))],
        out_specs=out_spec,
        out_shape=jax.ShapeDtypeStruct((c, r) if transpose else (r, c), BF16),
        compiler_params=_params("parallel"),
        name="cast_t_bf16" if transpose else "cast_bf16",
    )(x)


def peer_ffn_t(hn_t, layer, w_query, sub_keys, expert_u, expert_v):
    q_t = peer_query_t(w_query[layer].T.astype(BF16), hn_t)
    rank2, lr, e1, e2 = peer_select(q_t, sub_keys[layer])
    u = cast_bf16(expert_u, layer)
    v_t = cast_bf16(expert_v, layer, transpose=True)
    return peer_dense_t(hn_t, u, v_t, rank2, lr, e1, e2)


def kernel(x, mixer_norm_g, ffn_norm_g, attn_w_qkv, attn_w_o, attn_q_gain, attn_k_gain, conv_w_in, conv_w, conv_b,
           conv_w_out, peer_w_query, peer_sub_keys, peer_u, peer_v):
    batch, seq_len, d = x.shape
    h = x.reshape(batch * seq_len, d)
    depth = mixer_norm_g.shape[0]
    hn = rmsnorm(h, mixer_norm_g[0])
    for layer in range(depth):
        j = layer // 2
        if layer % 2 == 0:
            qkv = qkv_projection(hn, attn_w_qkv[j].astype(BF16), attn_q_gain[j], attn_k_gain[j], seq_len)
            o = attention(qkv, batch, seq_len, d)
            h = proj_residual(o, attn_w_o[j].astype(BF16), h)
        else:
            b, u = conv_in_projection(hn, conv_w_in[j].astype(BF16))
            y = gated_conv(b, u, conv_w[j], conv_b[j], seq_len)
            h = proj_residual(y, conv_w_out[j].astype(BF16), h)
        hn_t = rmsnorm_t(h, ffn_norm_g[layer])
        ffn_t = peer_ffn_t(hn_t, layer, peer_w_query, peer_sub_keys, peer_u, peer_v)
        if layer + 1 < depth:
            h, hn = add_rmsnorm(h, ffn_t, mixer_norm_g[layer + 1])
        else:
            h = add(h, ffn_t)
    return h.reshape(batch, seq_len, d)
```

```python
import functools

import numpy as np
import jax
import jax.numpy as jnp
from jax import lax
from jax.experimental import pallas as pl
from jax.experimental.pallas import tpu as pltpu

HEAD_DIM = 128
GQA_GROUP = 4
ROPE_THETA = 10000.0
GRID_W = 64
PEER_HEADS = 8
N_KEYS = 128
PEER_HALF = 128
PEER_TOPK = 16
EPS = 1e-6

LANES = 128
GATE_ROWS = 16
GATE_LANES = 2 * LANES
VMEM_LIMIT_BYTES = 56 * 2**20
F32 = jnp.float32
BF16 = jnp.bfloat16

_CAND = [(p1, p2) for p1 in range(PEER_TOPK) for p2 in range(PEER_TOPK // (p1 + 1))]
_CAND_ROWS = -(-len(_CAND) // 8) * 8


def _params(*semantics):
    return pltpu.CompilerParams(dimension_semantics=semantics, vmem_limit_bytes=VMEM_LIMIT_BYTES)


def _tile(n, pref):
    t = min(n, pref)
    assert n % t == 0, (n, pref)
    return t


def _rms(x, g):
    ms = jnp.mean(x * x, axis=-1, keepdims=True)
    return x * lax.rsqrt(ms + EPS) * g


def _rmsnorm_body(x_ref, g_ref, o_ref):
    o_ref[...] = _rms(x_ref[...], g_ref[...]).astype(o_ref.dtype)


def _add_rmsnorm_body(a_ref, bt_ref, g_ref, h_ref, o_ref):
    h = a_ref[...] + bt_ref[...].T
    h_ref[...] = h
    o_ref[...] = _rms(h, g_ref[...]).astype(o_ref.dtype)


def _add_body(a_ref, bt_ref, o_ref):
    o_ref[...] = a_ref[...] + bt_ref[...].T


def rmsnorm(x, g):
    t, d = x.shape
    tr = _tile(t, 256)
    row = pl.BlockSpec((tr, d), lambda i: (i, 0))
    return pl.pallas_call(
        _rmsnorm_body,
        grid=(t // tr,),
        in_specs=[row, pl.BlockSpec((1, d), lambda i: (0, 0))],
        out_specs=row,
        out_shape=jax.ShapeDtypeStruct((t, d), BF16),
        compiler_params=_params("parallel"),
        name="rmsnorm",
    )(x, g.reshape(1, d))


def _rmsnorm_t_body(x_ref, g_ref, o_ref):
    o_ref[...] = _rms(x_ref[...], g_ref[...]).T.astype(o_ref.dtype)


def rmsnorm_t(x, g):
    t, d = x.shape
    tr = _tile(t, 512)
    return pl.pallas_call(
        _rmsnorm_t_body,
        grid=(t // tr,),
        in_specs=[pl.BlockSpec((tr, d), lambda i: (i, 0)), pl.BlockSpec((1, d), lambda i: (0, 0))],
        out_specs=pl.BlockSpec((None, d, tr), lambda i: (i, 0, 0)),
        out_shape=jax.ShapeDtypeStruct((t // tr, d, tr), BF16),
        compiler_params=_params("parallel"),
        name="rmsnorm_t",
    )(x, g.reshape(1, d))


def add_rmsnorm(a, b_t, g):
    t, d = a.shape
    tr = _tile(t, 256)
    row = pl.BlockSpec((tr, d), lambda i: (i, 0))
    col = pl.BlockSpec((d, tr), lambda i: (0, i))
    return pl.pallas_call(
        _add_rmsnorm_body,
        grid=(t // tr,),
        in_specs=[row, col, pl.BlockSpec((1, d), lambda i: (0, 0))],
        out_specs=[row, row],
        out_shape=[jax.ShapeDtypeStruct((t, d), F32), jax.ShapeDtypeStruct((t, d), BF16)],
        compiler_params=_params("parallel"),
        name="add_rmsnorm",
    )(a, b_t, g.reshape(1, d))


def add(a, b_t):
    t, d = a.shape
    tr = _tile(t, 256)
    row = pl.BlockSpec((tr, d), lambda i: (i, 0))
    return pl.pallas_call(
        _add_body,
        grid=(t // tr,),
        in_specs=[row, pl.BlockSpec((d, tr), lambda i: (0, i))],
        out_specs=row,
        out_shape=jax.ShapeDtypeStruct((t, d), F32),
        compiler_params=_params("parallel"),
        name="residual_add",
    )(a, b_t)


def _rope_tables(seq_len):
    pos = np.arange(seq_len)
    row_idx = (pos // GRID_W).astype(np.float32)
    col_idx = (pos % GRID_W).astype(np.float32)
    axis_dim = HEAD_DIM // 2
    inv_freq = jnp.asarray(ROPE_THETA, F32) ** (-jnp.arange(0, axis_dim, 2, dtype=F32) / axis_dim)
    ang_r = jnp.asarray(row_idx)[:, None] * inv_freq
    ang_c = jnp.asarray(col_idx)[:, None] * inv_freq
    zeros = jnp.zeros_like(ang_r)
    cos = jnp.concatenate([jnp.cos(ang_r)] * 2 + [jnp.cos(ang_c)] * 2, axis=1)
    s_lo = jnp.concatenate([-jnp.sin(ang_r), zeros, -jnp.sin(ang_c), zeros], axis=1)
    s_hi = jnp.concatenate([zeros, jnp.sin(ang_r), zeros, jnp.sin(ang_c)], axis=1)
    return cos, s_lo, s_hi


def _qkv_body(x_ref, w_ref, gp_ref, c_ref, slo_ref, shi_ref, o_ref, *, n_qk_tiles, heads_per_tile):
    j = pl.program_id(1)
    acc = jnp.dot(x_ref[...], w_ref[...], preferred_element_type=F32)

    @pl.when(j < n_qk_tiles)
    def _():
        gain = gp_ref[0:1, :]
        post = gp_ref[1:2, :]
        cos, s_lo, s_hi = c_ref[...], slo_ref[...], shi_ref[...]
        for h in range(heads_per_tile):
            sl = slice(h * HEAD_DIM, (h + 1) * HEAD_DIM)
            y = _rms(acc[:, sl], gain)
            r = (y * cos + pltpu.roll(y, HEAD_DIM - 32, 1) * s_lo + pltpu.roll(y, 32, 1) * s_hi)
            o_ref[:, sl] = (r * post).astype(o_ref.dtype)

    @pl.when(j >= n_qk_tiles)
    def _():
        o_ref[...] = acc.astype(o_ref.dtype)


def qkv_projection(hn, w_qkv, q_gain, k_gain, seq_len):
    t, d = hn.shape
    n = w_qkv.shape[1]
    kv_cols = d // GQA_GROUP
    tm = _tile(seq_len, 1024)
    tn = _tile(kv_cols, 1024)
    n_q_tiles, n_k_tiles = d // tn, kv_cols // tn
    scale = HEAD_DIM ** -0.5 * np.log2(np.e)
    ones = jnp.ones((HEAD_DIM,), F32)
    rows = ([jnp.stack([q_gain, ones * scale])] * n_q_tiles + [jnp.stack([k_gain, ones])] * n_k_tiles
            + [jnp.stack([ones, ones])] * n_k_tiles)
    gp = jnp.stack(rows)
    cos, s_lo, s_hi = _rope_tables(seq_len)
    seq_tiles = seq_len // tm
    tab = pl.BlockSpec((tm, HEAD_DIM), lambda i, j: (i % seq_tiles, 0))
    body = functools.partial(_qkv_body, n_qk_tiles=n_q_tiles + n_k_tiles, heads_per_tile=tn // HEAD_DIM)
    return pl.pallas_call(
        body,
        grid=(t // tm, n // tn),
        in_specs=[
            pl.BlockSpec((tm, d), lambda i, j: (i, 0)),
            pl.BlockSpec((d, tn), lambda i, j: (0, j)),
            pl.BlockSpec((None, 2, HEAD_DIM), lambda i, j: (j, 0, 0)),
            tab, tab, tab,
        ],
        out_specs=pl.BlockSpec((tm, tn), lambda i, j: (i, j)),
        out_shape=jax.ShapeDtypeStruct((t, n), BF16),
        compiler_params=_params("parallel", "arbitrary"),
        name="qkv_projection",
    )(hn, w_qkv, gp, cos, s_lo, s_hi)


def _attention_body(q_ref, k_ref, v_ref, o_ref, vx_ref, *, rows_per_chain):
    @pl.when(pl.program_id(2) == 0)
    def _():
        vx_ref[:, 0:HEAD_DIM] = v_ref[...]
        vx_ref[:, HEAD_DIM:2 * HEAD_DIM] = jnp.ones(v_ref.shape, BF16)

    k = k_ref[...]
    for r0 in range(0, q_ref.shape[0], rows_per_chain):
        rows = slice(r0, r0 + rows_per_chain)
        for h in range(GQA_GROUP):
            sl = slice(h * HEAD_DIM, (h + 1) * HEAD_DIM)
            s = lax.dot_general(q_ref[rows, sl], k, (((1,), (1,)), ((), ())), preferred_element_type=F32)
            m = jnp.max(s, axis=-1, keepdims=True)
            p = jnp.exp2(s - m).astype(BF16)
            o = jnp.dot(p, vx_ref[...], preferred_element_type=F32)
            o_ref[rows, sl] = (o[:, 0:HEAD_DIM] / o[:, HEAD_DIM:2 * HEAD_DIM]).astype(o_ref.dtype)


def attention(qkv, batch, seq_len, d):
    t = qkv.shape[0]
    n_q = d // HEAD_DIM
    n_kv = n_q // GQA_GROUP
    tq = _tile(seq_len, 512)
    q_tiles = seq_len // tq
    gw = GQA_GROUP * HEAD_DIM
    body = functools.partial(_attention_body, rows_per_chain=_tile(tq, 128))
    return pl.pallas_call(
        body,
        grid=(batch, n_kv, q_tiles),
        in_specs=[
            pl.BlockSpec((tq, gw), lambda b, g, i: (b * q_tiles + i, g)),
            pl.BlockSpec((seq_len, HEAD_DIM), lambda b, g, i: (b, n_q + g)),
            pl.BlockSpec((seq_len, HEAD_DIM), lambda b, g, i: (b, n_q + n_kv + g)),
        ],
        out_specs=pl.BlockSpec((tq, gw), lambda b, g, i: (b * q_tiles + i, g)),
        out_shape=jax.ShapeDtypeStruct((t, d), BF16),
        scratch_shapes=[pltpu.VMEM((seq_len, 2 * HEAD_DIM), BF16)],
        compiler_params=_params("parallel", "parallel", "arbitrary"),
        name="attention",
    )(qkv, qkv, qkv)


def _proj_residual_body(x_ref, w_ref, r_ref, o_ref):
    o_ref[...] = r_ref[...] + jnp.dot(x_ref[...], w_ref[...], preferred_element_type=F32)


def proj_residual(x, w, res):
    t, k = x.shape
    n = w.shape[1]
    tm = _tile(t, 512)
    tn = _tile(n, 1024)
    return pl.pallas_call(
        _proj_residual_body,
        grid=(n // tn, t // tm),
        in_specs=[
            pl.BlockSpec((tm, k), lambda j, i: (i, 0)),
            pl.BlockSpec((k, tn), lambda j, i: (0, j)),
            pl.BlockSpec((tm, tn), lambda j, i: (i, j)),
        ],
        out_specs=pl.BlockSpec((tm, tn), lambda j, i: (i, j)),
        out_shape=jax.ShapeDtypeStruct((t, n), F32),
        compiler_params=_params("parallel", "arbitrary"),
        name="proj_residual",
    )(x, w, res)


def _conv_in_body(x_ref, wb_ref, wc_ref, wx_ref, b_ref, u_ref):
    x = x_ref[...]
    b_ref[...] = jnp.dot(x, wb_ref[...], preferred_element_type=F32).astype(b_ref.dtype)
    c = jnp.dot(x, wc_ref[...], preferred_element_type=F32)
    xin = jnp.dot(x, wx_ref[...], preferred_element_type=F32)
    u_ref[...] = (c * xin).astype(u_ref.dtype)


def conv_in_projection(hn, w_in):
    t, d = hn.shape
    c = w_in.shape[1] // 3
    tm = _tile(t, 512)
    tn = _tile(c, 512)
    nc = c // tn
    out = pl.BlockSpec((tm, tn), lambda i, j: (i, j))
    return pl.pallas_call(
        _conv_in_body,
        grid=(t // tm, nc),
        in_specs=[
            pl.BlockSpec((tm, d), lambda i, j: (i, 0)),
            pl.BlockSpec((d, tn), lambda i, j: (0, j)),
            pl.BlockSpec((d, tn), lambda i, j: (0, nc + j)),
            pl.BlockSpec((d, tn), lambda i, j: (0, 2 * nc + j)),
        ],
        out_specs=[out, out],
        out_shape=[jax.ShapeDtypeStruct((t, c), BF16)] * 2,
        compiler_params=_params("parallel", "arbitrary"),
        name="conv_in_projection",
    )(hn, w_in, w_in, w_in)


def _conv_body(b_ref, u_ref, w_ref, bias_ref, o_ref):
    u = u_ref[...].astype(F32)
    s = u.shape[0]
    pos = lax.broadcasted_iota(jnp.int32, u.shape, 0)
    prev = jnp.where(pos == 0, 0.0, pltpu.roll(u, 1, 0))
    nxt = jnp.where(pos == s - 1, 0.0, pltpu.roll(u, s - 1, 0))
    conv = prev * w_ref[0:1, :] + u * w_ref[1:2, :] + nxt * w_ref[2:3, :] + bias_ref[...]
    o_ref[...] = (b_ref[...].astype(F32) * conv).astype(o_ref.dtype)


def gated_conv(b, u, conv_w, conv_b, seq_len):
    t, c = u.shape
    tc = _tile(c, 256)
    blk = pl.BlockSpec((seq_len, tc), lambda i, j: (i, j))
    return pl.pallas_call(
        _conv_body,
        grid=(t // seq_len, c // tc),
        in_specs=[blk, blk, pl.BlockSpec((3, tc), lambda i, j: (0, j)), pl.BlockSpec((1, tc), lambda i, j: (0, j))],
        out_specs=blk,
        out_shape=jax.ShapeDtypeStruct((t, c), BF16),
        compiler_params=_params("parallel", "parallel"),
        name="gated_conv",
    )(b, u, conv_w, conv_b.reshape(1, c))


def _query_t_body(w_ref, x_ref, o_ref):
    o_ref[...] = jnp.dot(w_ref[...], x_ref[...], preferred_element_type=F32)


def peer_query_t(wq_t, hn_t):
    q, d = wq_t.shape
    n_slabs, _, tn = hn_t.shape
    t = n_slabs * tn
    tm = _tile(q, 1024)
    return pl.pallas_call(
        _query_t_body,
        grid=(q // tm, t // tn),
        in_specs=[pl.BlockSpec((tm, d), lambda i, j: (i, 0)), pl.BlockSpec((None, d, tn), lambda i, j: (j, 0, 0))],
        out_specs=pl.BlockSpec((tm, tn), lambda i, j: (i, j)),
        out_shape=jax.ShapeDtypeStruct((q, t), F32),
        compiler_params=_params("parallel", "arbitrary"),
        name="peer_query",
    )(wq_t, hn_t)


def _extract16(x, break_ties):
    n_rows = x.shape[0]
    iota = lax.broadcasted_iota(jnp.int32, x.shape, 0).astype(F32) if break_ties else None
    work = x
    rank = jnp.full(x.shape, float(PEER_TOPK), F32)
    vals = []
    for p in range(PEER_TOPK):
        m = jnp.max(work, axis=0, keepdims=True)
        hit = work == m
        if break_ties:
            hit = iota == jnp.min(jnp.where(hit, iota, float(n_rows)), axis=0, keepdims=True)
        rank = jnp.where(hit, float(p), rank)
        work = jnp.where(hit, -jnp.inf, work)
        vals.append(m)
    count = jnp.sum(jnp.where(rank < float(PEER_TOPK), 1.0, 0.0), axis=0, keepdims=True)
    return vals, rank, count


def _select_tile(s1, s2, outs, scratch, break_ties):
    tok_ref, rows_ref = outs
    cand_ref, sel_ref = scratch
    v1, rank1, n1 = _extract16(s1, break_ties)
    v2, rank2, n2 = _extract16(s2, break_ties)

    cand_ref[...] = jnp.full(cand_ref.shape, -jnp.inf, F32)
    for r, (p1, p2) in enumerate(_CAND):
        cand_ref[r:r + 1, :] = v1[p1] + v2[p2]
    cand = cand_ref[...]
    _, crank, n3 = _extract16(cand, break_ties)
    sel = jnp.where(crank < float(PEER_TOPK), 1.0, 0.0)
    z = jnp.sum(sel * jnp.exp(cand - cand[0:1, :]), axis=0, keepdims=True)
    sel_ref[...] = sel

    lr = jnp.zeros(s1.shape, F32)
    r0 = 0
    for p1 in range(PEER_TOPK):
        n = PEER_TOPK // (p1 + 1)
        count = jnp.sum(sel_ref[r0:r0 + n, :], axis=0, keepdims=True)
        lr = jnp.where(rank1 == float(p1), count, lr)
        r0 += n

    rows_ref[0] = lr
    rows_ref[1] = jnp.exp(s1 - v1[0])
    rank2 = rank2.astype(BF16)
    e2 = (jnp.exp(s2 - v2[0]) / z).astype(BF16)
    for r in range(N_KEYS // GATE_ROWS):
        rows = slice(r * GATE_ROWS, (r + 1) * GATE_ROWS)
        tok_ref[0, r] = rank2[rows, :]
        tok_ref[1, r] = e2[rows, :]
    full = float(PEER_TOPK)
    return jnp.where(n1 == full, 1.0, 0.0) + jnp.where(n2 == full, 1.0, 0.0) + jnp.where(n3 == full, 1.0, 0.0)


def _select_body(qt_ref, keys_ref, tok_ref, rows_ref, cand_ref, sel_ref):
    qt = qt_ref[...].astype(BF16)
    s1 = jnp.dot(keys_ref[0].astype(BF16), qt[:PEER_HALF], preferred_element_type=F32)
    s2 = jnp.dot(keys_ref[1].astype(BF16), qt[PEER_HALF:], preferred_element_type=F32)
    outs = (tok_ref, rows_ref)
    scratch = (cand_ref, sel_ref)
    clean = _select_tile(s1, s2, outs, scratch, break_ties=False)

    @pl.when(jnp.min(clean) < 3.0)
    def _():
        _select_tile(s1, s2, outs, scratch, break_ties=True)


def peer_select(q_t, sub_keys):
    t = q_t.shape[1]
    tb = GATE_LANES
    assert t % tb == 0
    n_r = N_KEYS // GATE_ROWS
    row_out = pl.BlockSpec((2, None, N_KEYS, tb), lambda h, i: (0, h, 0, i))
    row_shape = jax.ShapeDtypeStruct((2, PEER_HEADS, N_KEYS, t), F32)
    tile_out = pl.BlockSpec((2, None, n_r, None, GATE_ROWS, tb), lambda h, i: (0, i, 0, h, 0, 0))
    tile_shape = jax.ShapeDtypeStruct((2, t // tb, n_r, PEER_HEADS, GATE_ROWS, tb), BF16)
    return pl.pallas_call(
        _select_body,
        grid=(PEER_HEADS, t // tb),
        in_specs=[
            pl.BlockSpec((2 * PEER_HALF, tb), lambda h, i: (h, i)),
            pl.BlockSpec((None, 2, N_KEYS, PEER_HALF), lambda h, i: (h, 0, 0, 0)),
        ],
        out_specs=[tile_out, row_out],
        out_shape=[tile_shape, row_shape],
        scratch_shapes=[pltpu.VMEM((_CAND_ROWS, tb), F32), pltpu.VMEM((_CAND_ROWS, tb), F32)],
        compiler_params=_params("parallel", "arbitrary"),
        name="peer_select",
    )(q_t, sub_keys)


def _gelu(a):
    return 0.5 * a * (1.0 + lax.erf(a * np.float32(np.sqrt(0.5))))


def _gate_tile(tok_ref, rows_ref, g_ref, row0, anchor=None):
    zero = jnp.zeros((), BF16)
    piece = (GATE_ROWS, GATE_LANES)
    n_c, n_j = g_ref.shape[1] // GATE_LANES, g_ref.shape[0] // N_KEYS
    for c in range(n_c):
        cols = slice(c * GATE_LANES, (c + 1) * GATE_LANES)
        for j in range(n_j):
            row = pl.ds(row0 + j, 1)
            lrows = [rows_ref[0, h, row, cols] for h in range(PEER_HEADS)]
            if anchor is not None:
                a0 = (c * n_j + j) * (anchor.shape[0] // (n_c * n_j))
                bits = pltpu.bitcast(anchor[a0:a0 + 8, cols], jnp.uint32)
                zrow = pltpu.bitcast((bits >> 16) >> 16, F32)[0:1, :]
                lrows = [x + zrow for x in lrows]
            lrows = [jnp.broadcast_to(x, piece).astype(BF16) for x in lrows]
            erows = [jnp.broadcast_to(rows_ref[1, h, row, cols], piece).astype(BF16) for h in range(PEER_HEADS)]
            for r in range(N_KEYS // GATE_ROWS):
                gate = None
                for h in range(PEER_HEADS):
                    term = jnp.where(tok_ref[0, c, r, h] < lrows[h], tok_ref[1, c, r, h] * erows[h], zero)
                    gate = term if gate is None else gate + term
                g_ref[j * N_KEYS + r * GATE_ROWS:j * N_KEYS + (r + 1) * GATE_ROWS, cols] = gate


def _peer_dense_body(xn_ref, u_ref, vt_ref, tok_ref, rows_ref, o_ref, g_ref, pt_ref, *, n_tiles):
    e = pl.program_id(1)
    first_tile = e == 0

    @pl.when(first_tile)
    def _():
        o_ref[...] = jnp.zeros(o_ref.shape, F32)

    @pl.when(first_tile & (pl.program_id(0) == 0))
    def _():
        _gate_tile(tok_ref, rows_ref, g_ref, 0)

    n_sub = g_ref.shape[0] // N_KEYS
    tiles_per_block = rows_ref.shape[2] // n_sub
    a_t = jnp.dot(u_ref[...], xn_ref[...], preferred_element_type=F32)
    pt_ref[...] = g_ref[...] * _gelu(a_t).astype(BF16)
    half = g_ref.shape[0] // 2
    part0 = jnp.dot(vt_ref[:, :half], pt_ref[:half, :], preferred_element_type=F32)
    part1 = jnp.dot(vt_ref[:, half:], pt_ref[half:, :], preferred_element_type=F32)
    o_ref[...] += part0 + part1
    row0 = (((e + 1) % n_tiles) % tiles_per_block) * n_sub
    _gate_tile(tok_ref, rows_ref, g_ref, row0, anchor=part0)


def peer_dense_t(hn_t, u, v_t, tok, rows):
    n_slabs, d, tb = hn_t.shape
    t = n_slabs * tb
    n_exp = u.shape[0]
    eb = v_t.shape[2]
    assert v_t.shape == (n_exp // eb, d, eb)
    n_sub = eb // N_KEYS
    sublanes = 8
    tiles_per_block = sublanes // n_sub
    assert sublanes % n_sub == 0 and tiles_per_block > 1
    once = pl.Buffered(1)
    nt, ne = t // tb, n_exp // eb
    assert ne > 1

    def next_block(i, e):
        return jnp.minimum(i + (e + 1) // ne, nt - 1)

    tok_spec = pl.BlockSpec((2, tb // GATE_LANES) + tok.shape[2:], lambda i, e: (0, next_block(i, e), 0, 0, 0, 0),
                            pipeline_mode=once)
    rows_spec = pl.BlockSpec((2, PEER_HEADS, sublanes, tb),
                             lambda i, e: (0, 0, ((e + 1) % ne) // tiles_per_block, next_block(i, e)))
    return pl.pallas_call(
        functools.partial(_peer_dense_body, n_tiles=ne),
        grid=(nt, ne),
        in_specs=[
            pl.BlockSpec((None, d, tb), lambda i, e: (i, 0, 0), pipeline_mode=once),
            pl.BlockSpec((eb, d), lambda i, e: (e, 0)),
            pl.BlockSpec((None, d, eb), lambda i, e: (e, 0, 0)),
            tok_spec, rows_spec,
        ],
        out_specs=pl.BlockSpec((d, tb), lambda i, e: (0, i)),
        out_shape=jax.ShapeDtypeStruct((d, t), F32),
        scratch_shapes=[pltpu.VMEM((eb, tb), BF16), pltpu.VMEM((eb, tb), BF16)],
        compiler_params=_params("arbitrary", "arbitrary"),
        name="peer_dense",
    )(hn_t, u, v_t, tok, rows)


def _cast_body(x_ref, o_ref):
    o_ref[...] = x_ref[...].astype(o_ref.dtype)


def _cast_t_body(x_ref, o_ref):
    o_ref[...] = x_ref[...].T.astype(o_ref.dtype)


def cast_bf16(x, layer, transpose=False):
    _, r, c = x.shape
    tr = _tile(r, 512)
    if transpose:
        out_spec = pl.BlockSpec((None, c, tr), lambda i: (i, 0, 0))
        out_shape = jax.ShapeDtypeStruct((r // tr, c, tr), BF16)
    else:
        out_spec = pl.BlockSpec((tr, c), lambda i: (i, 0))
        out_shape = jax.ShapeDtypeStruct((r, c), BF16)
    return pl.pallas_call(
        _cast_t_body if transpose else _cast_body,
        grid=(r // tr,),
        in_specs=[pl.BlockSpec((None, tr, c), lambda i: (layer, i, 0))],
        out_specs=out_spec,
        out_shape=out_shape,
        compiler_params=_params("parallel"),
        name="cast_t_bf16" if transpose else "cast_bf16",
    )(x)


def peer_ffn_t(hn_t, layer, w_query, sub_keys, expert_u, expert_v):
    q_t = peer_query_t(w_query[layer].T.astype(BF16), hn_t)
    tok, rows = peer_select(q_t, sub_keys[layer])
    u = cast_bf16(expert_u, layer)
    v_t = cast_bf16(expert_v, layer, transpose=True)
    return peer_dense_t(hn_t, u, v_t, tok, rows)


def kernel(x, mixer_norm_g, ffn_norm_g, attn_w_qkv, attn_w_o, attn_q_gain, attn_k_gain, conv_w_in, conv_w, conv_b,
           conv_w_out, peer_w_query, peer_sub_keys, peer_u, peer_v):
    batch, seq_len, d = x.shape
    h = x.reshape(batch * seq_len, d)
    depth = mixer_norm_g.shape[0]
    hn = rmsnorm(h, mixer_norm_g[0])
    for layer in range(depth):
        j = layer // 2
        if layer % 2 == 0:
            qkv = qkv_projection(hn, attn_w_qkv[j].astype(BF16), attn_q_gain[j], attn_k_gain[j], seq_len)
            o = attention(qkv, batch, seq_len, d)
            h = proj_residual(o, attn_w_o[j].astype(BF16), h)
        else:
            b, u = conv_in_projection(hn, conv_w_in[j].astype(BF16))
            y = gated_conv(b, u, conv_w[j], conv_b[j], seq_len)
            h = proj_residual(y, conv_w_out[j].astype(BF16), h)
        hn_t = rmsnorm_t(h, ffn_norm_g[layer])
        ffn_t = peer_ffn_t(hn_t, layer, peer_w_query, peer_sub_keys, peer_u, peer_v)
        if layer + 1 < depth:
            h, hn = add_rmsnorm(h, ffn_t, mixer_norm_g[layer + 1])
        else:
            h = add(h, ffn_t)
    return h.reshape(batch, seq_len, d)
```

```python
import functools

import numpy as np
import jax
import jax.numpy as jnp
from jax import lax
from jax.experimental import pallas as pl
from jax.experimental.pallas import tpu as pltpu

HEAD_DIM = 128
GQA_GROUP = 4
ROPE_THETA = 10000.0
GRID_W = 64
PEER_HEADS = 8
N_KEYS = 128
PEER_HALF = 128
PEER_TOPK = 16
EPS = 1e-6

LANES = 128
GATE_ROWS = 16
GATE_LANES = 2 * LANES
DOT1_ROWS = 256
ANCHOR_BANDS = 4
VMEM_LIMIT_BYTES = 56 * 2**20
F32 = jnp.float32
BF16 = jnp.bfloat16

_CAND = [(p1, p2) for p1 in range(PEER_TOPK) for p2 in range(PEER_TOPK // (p1 + 1))]
_CAND_ROWS = -(-len(_CAND) // 8) * 8


def _params(*semantics):
    return pltpu.CompilerParams(dimension_semantics=semantics, vmem_limit_bytes=VMEM_LIMIT_BYTES)


def _tile(n, pref):
    t = min(n, pref)
    assert n % t == 0, (n, pref)
    return t


def _rms(x, g):
    ms = jnp.mean(x * x, axis=-1, keepdims=True)
    return x * lax.rsqrt(ms + EPS) * g


def _rmsnorm_body(x_ref, g_ref, o_ref):
    o_ref[...] = _rms(x_ref[...], g_ref[...]).astype(o_ref.dtype)


def _add_rmsnorm_body(a_ref, bt_ref, g_ref, h_ref, o_ref):
    h = a_ref[...] + bt_ref[...].T
    h_ref[...] = h
    o_ref[...] = _rms(h, g_ref[...]).astype(o_ref.dtype)


def _add_body(a_ref, bt_ref, o_ref):
    o_ref[...] = a_ref[...] + bt_ref[...].T


def rmsnorm(x, g):
    t, d = x.shape
    tr = _tile(t, 256)
    row = pl.BlockSpec((tr, d), lambda i: (i, 0))
    return pl.pallas_call(
        _rmsnorm_body,
        grid=(t // tr,),
        in_specs=[row, pl.BlockSpec((1, d), lambda i: (0, 0))],
        out_specs=row,
        out_shape=jax.ShapeDtypeStruct((t, d), BF16),
        compiler_params=_params("parallel"),
        name="rmsnorm",
    )(x, g.reshape(1, d))


def _rmsnorm_t_body(x_ref, g_ref, o_ref):
    o_ref[...] = _rms(x_ref[...], g_ref[...]).T.astype(o_ref.dtype)


def rmsnorm_t(x, g):
    t, d = x.shape
    tr = _tile(t, 512)
    return pl.pallas_call(
        _rmsnorm_t_body,
        grid=(t // tr,),
        in_specs=[pl.BlockSpec((tr, d), lambda i: (i, 0)), pl.BlockSpec((1, d), lambda i: (0, 0))],
        out_specs=pl.BlockSpec((None, d, tr), lambda i: (i, 0, 0)),
        out_shape=jax.ShapeDtypeStruct((t // tr, d, tr), BF16),
        compiler_params=_params("parallel"),
        name="rmsnorm_t",
    )(x, g.reshape(1, d))


def add_rmsnorm(a, b_t, g):
    t, d = a.shape
    tr = _tile(t, 256)
    row = pl.BlockSpec((tr, d), lambda i: (i, 0))
    col = pl.BlockSpec((d, tr), lambda i: (0, i))
    return pl.pallas_call(
        _add_rmsnorm_body,
        grid=(t // tr,),
        in_specs=[row, col, pl.BlockSpec((1, d), lambda i: (0, 0))],
        out_specs=[row, row],
        out_shape=[jax.ShapeDtypeStruct((t, d), F32), jax.ShapeDtypeStruct((t, d), BF16)],
        compiler_params=_params("parallel"),
        name="add_rmsnorm",
    )(a, b_t, g.reshape(1, d))


def add(a, b_t):
    t, d = a.shape
    tr = _tile(t, 256)
    row = pl.BlockSpec((tr, d), lambda i: (i, 0))
    return pl.pallas_call(
        _add_body,
        grid=(t // tr,),
        in_specs=[row, pl.BlockSpec((d, tr), lambda i: (0, i))],
        out_specs=row,
        out_shape=jax.ShapeDtypeStruct((t, d), F32),
        compiler_params=_params("parallel"),
        name="residual_add",
    )(a, b_t)


def _rope_tables(seq_len):
    pos = np.arange(seq_len)
    row_idx = (pos // GRID_W).astype(np.float32)
    col_idx = (pos % GRID_W).astype(np.float32)
    axis_dim = HEAD_DIM // 2
    inv_freq = jnp.asarray(ROPE_THETA, F32) ** (-jnp.arange(0, axis_dim, 2, dtype=F32) / axis_dim)
    ang_r = jnp.asarray(row_idx)[:, None] * inv_freq
    ang_c = jnp.asarray(col_idx)[:, None] * inv_freq
    zeros = jnp.zeros_like(ang_r)
    cos = jnp.concatenate([jnp.cos(ang_r)] * 2 + [jnp.cos(ang_c)] * 2, axis=1)
    s_lo = jnp.concatenate([-jnp.sin(ang_r), zeros, -jnp.sin(ang_c), zeros], axis=1)
    s_hi = jnp.concatenate([zeros, jnp.sin(ang_r), zeros, jnp.sin(ang_c)], axis=1)
    return cos, s_lo, s_hi


def _qkv_body(x_ref, w_ref, gp_ref, c_ref, slo_ref, shi_ref, o_ref, *, n_qk_tiles, heads_per_tile):
    j = pl.program_id(1)
    acc = jnp.dot(x_ref[...], w_ref[...], preferred_element_type=F32)

    @pl.when(j < n_qk_tiles)
    def _():
        gain = gp_ref[0:1, :]
        post = gp_ref[1:2, :]
        cos, s_lo, s_hi = c_ref[...], slo_ref[...], shi_ref[...]
        for h in range(heads_per_tile):
            sl = slice(h * HEAD_DIM, (h + 1) * HEAD_DIM)
            y = _rms(acc[:, sl], gain)
            r = (y * cos + pltpu.roll(y, HEAD_DIM - 32, 1) * s_lo + pltpu.roll(y, 32, 1) * s_hi)
            o_ref[:, sl] = (r * post).astype(o_ref.dtype)

    @pl.when(j >= n_qk_tiles)
    def _():
        o_ref[...] = acc.astype(o_ref.dtype)


def qkv_projection(hn, w_qkv, q_gain, k_gain, seq_len):
    t, d = hn.shape
    n = w_qkv.shape[1]
    kv_cols = d // GQA_GROUP
    tm = _tile(seq_len, 1024)
    tn = _tile(kv_cols, 1024)
    n_q_tiles, n_k_tiles = d // tn, kv_cols // tn
    scale = HEAD_DIM ** -0.5 * np.log2(np.e)
    ones = jnp.ones((HEAD_DIM,), F32)
    rows = ([jnp.stack([q_gain, ones * scale])] * n_q_tiles + [jnp.stack([k_gain, ones])] * n_k_tiles
            + [jnp.stack([ones, ones])] * n_k_tiles)
    gp = jnp.stack(rows)
    cos, s_lo, s_hi = _rope_tables(seq_len)
    seq_tiles = seq_len // tm
    tab = pl.BlockSpec((tm, HEAD_DIM), lambda i, j: (i % seq_tiles, 0))
    body = functools.partial(_qkv_body, n_qk_tiles=n_q_tiles + n_k_tiles, heads_per_tile=tn // HEAD_DIM)
    return pl.pallas_call(
        body,
        grid=(t // tm, n // tn),
        in_specs=[
            pl.BlockSpec((tm, d), lambda i, j: (i, 0)),
            pl.BlockSpec((d, tn), lambda i, j: (0, j)),
            pl.BlockSpec((None, 2, HEAD_DIM), lambda i, j: (j, 0, 0)),
            tab, tab, tab,
        ],
        out_specs=pl.BlockSpec((tm, tn), lambda i, j: (i, j)),
        out_shape=jax.ShapeDtypeStruct((t, n), BF16),
        compiler_params=_params("parallel", "arbitrary"),
        name="qkv_projection",
    )(hn, w_qkv, gp, cos, s_lo, s_hi)


def _attention_body(q_ref, k_ref, v_ref, o_ref, vx_ref, *, rows_per_chain):
    @pl.when(pl.program_id(2) == 0)
    def _():
        vx_ref[:, 0:HEAD_DIM] = v_ref[...]
        vx_ref[:, HEAD_DIM:2 * HEAD_DIM] = jnp.ones(v_ref.shape, BF16)

    k = k_ref[...]
    for r0 in range(0, q_ref.shape[0], rows_per_chain):
        rows = slice(r0, r0 + rows_per_chain)
        for h in range(GQA_GROUP):
            sl = slice(h * HEAD_DIM, (h + 1) * HEAD_DIM)
            s = lax.dot_general(q_ref[rows, sl], k, (((1,), (1,)), ((), ())), preferred_element_type=F32)
            m = jnp.max(s, axis=-1, keepdims=True)
            p = jnp.exp2(s - m).astype(BF16)
            o = jnp.dot(p, vx_ref[...], preferred_element_type=F32)
            o_ref[rows, sl] = (o[:, 0:HEAD_DIM] / o[:, HEAD_DIM:2 * HEAD_DIM]).astype(o_ref.dtype)


def attention(qkv, batch, seq_len, d):
    t = qkv.shape[0]
    n_q = d // HEAD_DIM
    n_kv = n_q // GQA_GROUP
    tq = _tile(seq_len, 512)
    q_tiles = seq_len // tq
    gw = GQA_GROUP * HEAD_DIM
    body = functools.partial(_attention_body, rows_per_chain=_tile(tq, 128))
    return pl.pallas_call(
        body,
        grid=(batch, n_kv, q_tiles),
        in_specs=[
            pl.BlockSpec((tq, gw), lambda b, g, i: (b * q_tiles + i, g)),
            pl.BlockSpec((seq_len, HEAD_DIM), lambda b, g, i: (b, n_q + g)),
            pl.BlockSpec((seq_len, HEAD_DIM), lambda b, g, i: (b, n_q + n_kv + g)),
        ],
        out_specs=pl.BlockSpec((tq, gw), lambda b, g, i: (b * q_tiles + i, g)),
        out_shape=jax.ShapeDtypeStruct((t, d), BF16),
        scratch_shapes=[pltpu.VMEM((seq_len, 2 * HEAD_DIM), BF16)],
        compiler_params=_params("parallel", "parallel", "arbitrary"),
        name="attention",
    )(qkv, qkv, qkv)


def _proj_residual_body(x_ref, w_ref, r_ref, o_ref):
    o_ref[...] = r_ref[...] + jnp.dot(x_ref[...], w_ref[...], preferred_element_type=F32)


def proj_residual(x, w, res):
    t, k = x.shape
    n = w.shape[1]
    tm = _tile(t, 512)
    tn = _tile(n, 1024)
    return pl.pallas_call(
        _proj_residual_body,
        grid=(n // tn, t // tm),
        in_specs=[
            pl.BlockSpec((tm, k), lambda j, i: (i, 0)),
            pl.BlockSpec((k, tn), lambda j, i: (0, j)),
            pl.BlockSpec((tm, tn), lambda j, i: (i, j)),
        ],
        out_specs=pl.BlockSpec((tm, tn), lambda j, i: (i, j)),
        out_shape=jax.ShapeDtypeStruct((t, n), F32),
        compiler_params=_params("parallel", "arbitrary"),
        name="proj_residual",
    )(x, w, res)


def _conv_in_body(x_ref, wb_ref, wc_ref, wx_ref, b_ref, u_ref):
    x = x_ref[...]
    b_ref[...] = jnp.dot(x, wb_ref[...], preferred_element_type=F32).astype(b_ref.dtype)
    c = jnp.dot(x, wc_ref[...], preferred_element_type=F32)
    xin = jnp.dot(x, wx_ref[...], preferred_element_type=F32)
    u_ref[...] = (c * xin).astype(u_ref.dtype)


def conv_in_projection(hn, w_in):
    t, d = hn.shape
    c = w_in.shape[1] // 3
    tm = _tile(t, 512)
    tn = _tile(c, 512)
    nc = c // tn
    out = pl.BlockSpec((tm, tn), lambda i, j: (i, j))
    return pl.pallas_call(
        _conv_in_body,
        grid=(t // tm, nc),
        in_specs=[
            pl.BlockSpec((tm, d), lambda i, j: (i, 0)),
            pl.BlockSpec((d, tn), lambda i, j: (0, j)),
            pl.BlockSpec((d, tn), lambda i, j: (0, nc + j)),
            pl.BlockSpec((d, tn), lambda i, j: (0, 2 * nc + j)),
        ],
        out_specs=[out, out],
        out_shape=[jax.ShapeDtypeStruct((t, c), BF16)] * 2,
        compiler_params=_params("parallel", "arbitrary"),
        name="conv_in_projection",
    )(hn, w_in, w_in, w_in)


def _conv_body(b_ref, u_ref, w_ref, bias_ref, o_ref):
    u = u_ref[...].astype(F32)
    s = u.shape[0]
    pos = lax.broadcasted_iota(jnp.int32, u.shape, 0)
    prev = jnp.where(pos == 0, 0.0, pltpu.roll(u, 1, 0))
    nxt = jnp.where(pos == s - 1, 0.0, pltpu.roll(u, s - 1, 0))
    conv = prev * w_ref[0:1, :] + u * w_ref[1:2, :] + nxt * w_ref[2:3, :] + bias_ref[...]
    o_ref[...] = (b_ref[...].astype(F32) * conv).astype(o_ref.dtype)


def gated_conv(b, u, conv_w, conv_b, seq_len):
    t, c = u.shape
    tc = _tile(c, 256)
    blk = pl.BlockSpec((seq_len, tc), lambda i, j: (i, j))
    return pl.pallas_call(
        _conv_body,
        grid=(t // seq_len, c // tc),
        in_specs=[blk, blk, pl.BlockSpec((3, tc), lambda i, j: (0, j)), pl.BlockSpec((1, tc), lambda i, j: (0, j))],
        out_specs=blk,
        out_shape=jax.ShapeDtypeStruct((t, c), BF16),
        compiler_params=_params("parallel", "parallel"),
        name="gated_conv",
    )(b, u, conv_w, conv_b.reshape(1, c))


def _query_t_body(w_ref, x_ref, o_ref):
    o_ref[...] = jnp.dot(w_ref[...], x_ref[...], preferred_element_type=F32)


def peer_query_t(wq_t, hn_t):
    q, d = wq_t.shape
    n_slabs, _, tn = hn_t.shape
    t = n_slabs * tn
    tm = _tile(q, 1024)
    return pl.pallas_call(
        _query_t_body,
        grid=(q // tm, t // tn),
        in_specs=[pl.BlockSpec((tm, d), lambda i, j: (i, 0)), pl.BlockSpec((None, d, tn), lambda i, j: (j, 0, 0))],
        out_specs=pl.BlockSpec((tm, tn), lambda i, j: (i, j)),
        out_shape=jax.ShapeDtypeStruct((q, t), F32),
        compiler_params=_params("parallel", "arbitrary"),
        name="peer_query",
    )(wq_t, hn_t)


def _extract16(x, break_ties):
    n_rows = x.shape[0]
    iota = lax.broadcasted_iota(jnp.int32, x.shape, 0).astype(F32) if break_ties else None
    work = x
    rank = jnp.full(x.shape, float(PEER_TOPK), F32)
    vals = []
    for p in range(PEER_TOPK):
        m = jnp.max(work, axis=0, keepdims=True)
        hit = work == m
        if break_ties:
            hit = iota == jnp.min(jnp.where(hit, iota, float(n_rows)), axis=0, keepdims=True)
        rank = jnp.where(hit, float(p), rank)
        work = jnp.where(hit, -jnp.inf, work)
        vals.append(m)
    count = jnp.sum(jnp.where(rank < float(PEER_TOPK), 1.0, 0.0), axis=0, keepdims=True)
    return vals, rank, count


def _select_tile(s1, s2, outs, scratch, break_ties):
    tok_ref, rows_ref = outs
    cand_ref, sel_ref = scratch
    v1, rank1, n1 = _extract16(s1, break_ties)
    v2, rank2, n2 = _extract16(s2, break_ties)

    cand_ref[...] = jnp.full(cand_ref.shape, -jnp.inf, F32)
    for r, (p1, p2) in enumerate(_CAND):
        cand_ref[r:r + 1, :] = v1[p1] + v2[p2]
    cand = cand_ref[...]
    _, crank, n3 = _extract16(cand, break_ties)
    sel = jnp.where(crank < float(PEER_TOPK), 1.0, 0.0)
    z = jnp.sum(sel * jnp.exp(cand - cand[0:1, :]), axis=0, keepdims=True)
    sel_ref[...] = sel

    lr = jnp.zeros(s1.shape, F32)
    r0 = 0
    for p1 in range(PEER_TOPK):
        n = PEER_TOPK // (p1 + 1)
        count = jnp.sum(sel_ref[r0:r0 + n, :], axis=0, keepdims=True)
        lr = jnp.where(rank1 == float(p1), count, lr)
        r0 += n

    rows_ref[0] = lr
    rows_ref[1] = jnp.exp(s1 - v1[0])
    rank2 = rank2.astype(BF16)
    e2 = (jnp.exp(s2 - v2[0]) / z).astype(BF16)
    for r in range(N_KEYS // GATE_ROWS):
        rows = slice(r * GATE_ROWS, (r + 1) * GATE_ROWS)
        tok_ref[0, r] = rank2[rows, :]
        tok_ref[1, r] = e2[rows, :]
    full = float(PEER_TOPK)
    return jnp.where(n1 == full, 1.0, 0.0) + jnp.where(n2 == full, 1.0, 0.0) + jnp.where(n3 == full, 1.0, 0.0)


def _select_body(qt_ref, keys_ref, tok_ref, rows_ref, cand_ref, sel_ref):
    qt = qt_ref[...].astype(BF16)
    s1 = jnp.dot(keys_ref[0].astype(BF16), qt[:PEER_HALF], preferred_element_type=F32)
    s2 = jnp.dot(keys_ref[1].astype(BF16), qt[PEER_HALF:], preferred_element_type=F32)
    outs = (tok_ref, rows_ref)
    scratch = (cand_ref, sel_ref)
    clean = _select_tile(s1, s2, outs, scratch, break_ties=False)

    @pl.when(jnp.min(clean) < 3.0)
    def _():
        _select_tile(s1, s2, outs, scratch, break_ties=True)


def peer_select(q_t, sub_keys):
    t = q_t.shape[1]
    tb = GATE_LANES
    assert t % tb == 0
    n_r = N_KEYS // GATE_ROWS
    row_out = pl.BlockSpec((2, None, N_KEYS, tb), lambda h, i: (0, h, 0, i))
    row_shape = jax.ShapeDtypeStruct((2, PEER_HEADS, N_KEYS, t), F32)
    tile_out = pl.BlockSpec((2, None, n_r, None, GATE_ROWS, tb), lambda h, i: (0, i, 0, h, 0, 0))
    tile_shape = jax.ShapeDtypeStruct((2, t // tb, n_r, PEER_HEADS, GATE_ROWS, tb), BF16)
    return pl.pallas_call(
        _select_body,
        grid=(PEER_HEADS, t // tb),
        in_specs=[
            pl.BlockSpec((2 * PEER_HALF, tb), lambda h, i: (h, i)),
            pl.BlockSpec((None, 2, N_KEYS, PEER_HALF), lambda h, i: (h, 0, 0, 0)),
        ],
        out_specs=[tile_out, row_out],
        out_shape=[tile_shape, row_shape],
        scratch_shapes=[pltpu.VMEM((_CAND_ROWS, tb), F32), pltpu.VMEM((_CAND_ROWS, tb), F32)],
        compiler_params=_params("parallel", "arbitrary"),
        name="peer_select",
    )(q_t, sub_keys)


def _gelu(a):
    return 0.5 * a * (1.0 + lax.erf(a * np.float32(np.sqrt(0.5))))


def _gate_tile(tok_ref, rows_ref, g_ref, row0, anchor=None):
    zero = jnp.zeros((), BF16)
    piece = (GATE_ROWS, GATE_LANES)
    n_c, n_j = g_ref.shape[1] // GATE_LANES, g_ref.shape[0] // N_KEYS
    for c in range(n_c):
        cols = slice(c * GATE_LANES, (c + 1) * GATE_LANES)
        for j in range(n_j):
            row = pl.ds(row0 + j, 1)
            lrows_f32 = [rows_ref[0, h, row, cols] for h in range(PEER_HEADS)]
            erows = [jnp.broadcast_to(rows_ref[1, h, row, cols], piece).astype(BF16) for h in range(PEER_HEADS)]
            n_r = N_KEYS // GATE_ROWS
            n_bands = ANCHOR_BANDS if anchor is not None else 1
            for band in range(n_bands):
                lrows = lrows_f32
                if anchor is not None:
                    a0 = ((c * n_j + j) * n_bands + band) * (anchor.shape[0] // (n_c * n_j * n_bands))
                    bits = pltpu.bitcast(anchor[a0:a0 + 8, cols], jnp.uint32)
                    zrow = pltpu.bitcast((bits >> 16) >> 16, F32)[0:1, :]
                    lrows = [x + zrow for x in lrows]
                lrows = [jnp.broadcast_to(x, piece).astype(BF16) for x in lrows]
                for r in range(band * n_r // n_bands, (band + 1) * n_r // n_bands):
                    gate = None
                    for h in range(PEER_HEADS):
                        term = jnp.where(tok_ref[0, c, r, h] < lrows[h], tok_ref[1, c, r, h] * erows[h], zero)
                        gate = term if gate is None else gate + term
                    g_ref[j * N_KEYS + r * GATE_ROWS:j * N_KEYS + (r + 1) * GATE_ROWS, cols] = gate


def _peer_dense_body(xn_ref, u_ref, vt_ref, tok_ref, rows_ref, o_ref, g_ref, pt_ref, *, n_tiles):
    e = pl.program_id(1)
    first_tile = e == 0

    @pl.when(first_tile)
    def _():
        o_ref[...] = jnp.zeros(o_ref.shape, F32)

    @pl.when(first_tile & (pl.program_id(0) == 0))
    def _():
        _gate_tile(tok_ref, rows_ref, g_ref, 0)

    n_sub = g_ref.shape[0] // N_KEYS
    tiles_per_block = rows_ref.shape[2] // n_sub
    half = g_ref.shape[0] // 2
    for r0 in range(0, 2 * half, DOT1_ROWS):
        rows = slice(r0, r0 + DOT1_ROWS)
        a_t = jnp.dot(u_ref[rows, :], xn_ref[...], preferred_element_type=F32)
        pt_ref[rows, :] = g_ref[rows, :] * _gelu(a_t).astype(BF16)
    part0 =jnp.dot(vt_ref[:, :half], pt_ref[:half, :], preferred_element_type=F32)
    part1 = jnp.dot(vt_ref[:, half:], pt_ref[half:, :], preferred_element_type=F32)
    o_ref[...] += part0 + part1
    row0 = (((e + 1) % n_tiles) % tiles_per_block) * n_sub
    _gate_tile(tok_ref, rows_ref, g_ref, row0, anchor=part0)


def peer_dense_t(hn_t, u, v_t, tok, rows):
    n_slabs, d, tb = hn_t.shape
    t = n_slabs * tb
    n_exp = u.shape[0]
    eb = v_t.shape[2]
    assert v_t.shape == (n_exp // eb, d, eb)
    n_sub = eb // N_KEYS
    sublanes = 8
    tiles_per_block = sublanes // n_sub
    assert sublanes % n_sub == 0 and tiles_per_block > 1
    once = pl.Buffered(1)
    nt, ne = t // tb, n_exp // eb
    assert ne > 1

    def next_block(i, e):
        return jnp.minimum(i + (e + 1) // ne, nt - 1)

    tok_spec = pl.BlockSpec((2, tb // GATE_LANES) + tok.shape[2:], lambda i, e: (0, next_block(i, e), 0, 0, 0, 0),
                            pipeline_mode=once)
    rows_spec = pl.BlockSpec((2, PEER_HEADS, sublanes, tb),
                             lambda i, e: (0, 0, ((e + 1) % ne) // tiles_per_block, next_block(i, e)))
    return pl.pallas_call(
        functools.partial(_peer_dense_body, n_tiles=ne),
        grid=(nt, ne),
        in_specs=[
            pl.BlockSpec((None, d, tb), lambda i, e: (i, 0, 0), pipeline_mode=once),
            pl.BlockSpec((eb, d), lambda i, e: (e, 0)),
            pl.BlockSpec((None, d, eb), lambda i, e: (e, 0, 0)),
            tok_spec, rows_spec,
        ],
        out_specs=pl.BlockSpec((d, tb), lambda i, e: (0, i)),
        out_shape=jax.ShapeDtypeStruct((d, t), F32),
        scratch_shapes=[pltpu.VMEM((eb, tb), BF16), pltpu.VMEM((eb, tb), BF16)],
        compiler_params=_params("arbitrary", "arbitrary"),
        name="peer_dense",
    )(hn_t, u, v_t, tok, rows)


def _cast_body(x_ref, o_ref):
    o_ref[...] = x_ref[...].astype(o_ref.dtype)


def _cast_t_body(x_ref, o_ref):
    o_ref[...] = x_ref[...].T.astype(o_ref.dtype)


def cast_bf16(x, layer, transpose=False):
    _, r, c = x.shape
    tr = _tile(r, 512)
    if transpose:
        out_spec = pl.BlockSpec((None, c, tr), lambda i: (i, 0, 0))
        out_shape = jax.ShapeDtypeStruct((r // tr, c, tr), BF16)
    else:
        out_spec = pl.BlockSpec((tr, c), lambda i: (i, 0))
        out_shape = jax.ShapeDtypeStruct((r, c), BF16)
    return pl.pallas_call(
        _cast_t_body if transpose else _cast_body,
        grid=(r // tr,),
        in_specs=[pl.BlockSpec((None, tr, c), lambda i: (layer, i, 0))],
        out_specs=out_spec,
        out_shape=out_shape,
        compiler_params=_params("parallel"),
        name="cast_t_bf16" if transpose else "cast_bf16",
    )(x)


def peer_ffn_t(hn_t, layer, w_query, sub_keys, expert_u, expert_v):
    q_t = peer_query_t(w_query[layer].T.astype(BF16), hn_t)
    tok, rows = peer_select(q_t, sub_keys[layer])
    u = cast_bf16(expert_u, layer)
    v_t = cast_bf16(expert_v, layer, transpose=True)
    return peer_dense_t(hn_t, u, v_t, tok, rows)


def kernel(x, mixer_norm_g, ffn_norm_g, attn_w_qkv, attn_w_o, attn_q_gain, attn_k_gain, conv_w_in, conv_w, conv_b,
           conv_w_out, peer_w_query, peer_sub_keys, peer_u, peer_v):
    batch, seq_len, d = x.shape
    h = x.reshape(batch * seq_len, d)
    depth = mixer_norm_g.shape[0]
    hn = rmsnorm(h, mixer_norm_g[0])
    for layer in range(depth):
        j = layer // 2
        if layer % 2 == 0:
            qkv = qkv_projection(hn, attn_w_qkv[j].astype(BF16), attn_q_gain[j], attn_k_gain[j], seq_len)
            o = attention(qkv, batch, seq_len, d)
            h = proj_residual(o, attn_w_o[j].astype(BF16), h)
        else:
            b, u = conv_in_projection(hn, conv_w_in[j].astype(BF16))
            y = gated_conv(b, u, conv_w[j], conv_b[j], seq_len)
            h = proj_residual(y, conv_w_out[j].astype(BF16), h)
        hn_t = rmsnorm_t(h, ffn_norm_g[layer])
        ffn_t = peer_ffn_t(hn_t, layer, peer_w_query, peer_sub_keys, peer_u, peer_v)
        if layer + 1 < depth:
            h, hn = add_rmsnorm(h, ffn_t, mixer_norm_g[layer + 1])
        else:
            h = add(h, ffn_t)
    return h.reshape(batch, seq_len, d)
```
